```python
import jax, jax.numpy as jnp
from jax import lax
import numpy as np

D_MODEL = 4096
BATCH = 32
SEQ = 256
DEPTH = 2
DEC_BATCH = 2
DEC_SEQ = 1024
PAST_LEN = 512

GRID_W = 64
N_BRANCH = 4
BRANCH_W = D_MODEL // 4
FNET_GROUPS = 4
FNET_GW = BRANCH_W // FNET_GROUPS
RWKV_HEAD = 64
RWKV_H = BRANCH_W // RWKV_HEAD
RWKV_W_RANK = 64
RWKV_A_RANK = 64
RWKV_V_RANK = 32
RWKV_G_RANK = 160
RWKV_GN_EPS = 64e-5
RWKV_IN = 3 * BRANCH_W + 2 * RWKV_W_RANK + 2 * RWKV_A_RANK + RWKV_G_RANK
RWKV_SPLITS = (BRANCH_W, 2 * BRANCH_W, 3 * BRANCH_W,
               3 * BRANCH_W + 2 * RWKV_W_RANK,
               3 * BRANCH_W + 2 * RWKV_W_RANK + 2 * RWKV_A_RANK)
LRU_W = BRANCH_W
LRU_BLOCKS = 4
LRU_BW = LRU_W // LRU_BLOCKS
LRU_CONV = 4
LRU_CONV_LEFT = 1
LRU_C = 8.0
POOL_WINDOWS = (2, 4, 8, 16)
POOL_GW = BRANCH_W // len(POOL_WINDOWS)
P_IN = BRANCH_W + RWKV_IN + 2 * LRU_W + BRANCH_W + N_BRANCH * D_MODEL
W_IN_SPLITS = (BRANCH_W, BRANCH_W + RWKV_IN, BRANCH_W + RWKV_IN + LRU_W,
               BRANCH_W + RWKV_IN + 2 * LRU_W, 2 * BRANCH_W + RWKV_IN + 2 * LRU_W)
N_EXPERTS = 32
TOP_K = 4
EXPERT_FF = D_MODEL // 2
SWIGLU_LIMIT = 7.0
SWIGLU_ALPHA = 1.702
MOE_BLOCK = 256
RMS_EPS = 1e-6
POS_BASE = 10000.0

kernel_name = "hybrid_flow_fourier_rwkv7_rglru_pool_moe_step"


def rmsnorm(x, g):
    xf = x.astype(jnp.float32)
    y = xf * lax.rsqrt(jnp.mean(xf * xf, axis=-1, keepdims=True) + RMS_EPS)
    return (y * g.astype(jnp.float32)).astype(x.dtype)


def grid_position_code(n_tokens):
    rows = n_tokens // GRID_W
    rr, cc = jnp.meshgrid(jnp.arange(rows, dtype=jnp.float32),
                          jnp.arange(GRID_W, dtype=jnp.float32), indexing="ij")
    quarter = D_MODEL // 4
    omega = POS_BASE ** (-jnp.arange(quarter, dtype=jnp.float32) / quarter)

    def enc(pos):
        ang = pos.reshape(-1)[:, None] * omega[None, :]
        return jnp.concatenate([jnp.sin(ang), jnp.cos(ang)], axis=-1)

    return jnp.concatenate([enc(rr), enc(cc)], axis=-1)


def token_shift_centred(z):
    zp = jnp.pad(z, ((0, 0), (1, 1), (0, 0)))
    return 0.5 * (zp[:, :-2] + zp[:, 2:])


def fourier_mixer(z, w):
    B, T, _ = z.shape
    zg = z.astype(jnp.float32).reshape(B, T, FNET_GROUPS, FNET_GW)
    f = jnp.fft.fft2(zg, axes=(1, 3), norm="ortho").real
    y = jnp.einsum("btgc,gcd->btgd", f, w.astype(jnp.float32))
    return y.reshape(B, T, BRANCH_W)


def pool_mixer(z, w, scale):
    B, T, _ = z.shape
    n_g = len(POOL_WINDOWS)
    zg = z.astype(jnp.float32).reshape(B, T, n_g, POOL_GW)
    cs = jnp.concatenate([jnp.zeros((B, 1, n_g, POOL_GW), jnp.float32),
                          jnp.cumsum(zg, axis=1)], axis=1)
    t = jnp.arange(T)
    pooled = []
    for gi, win in enumerate(POOL_WINDOWS):
        lo = jnp.clip(t - win // 2, 0, T)
        hi = jnp.clip(t + win // 2, 0, T)
        csg = cs[:, :, gi]
        cnt = (hi - lo).astype(jnp.float32)
        pooled.append((csg[:, hi] - csg[:, lo]) / cnt[None, :, None])
    d = jnp.stack(pooled, axis=2) - zg
    y = jnp.einsum("btgc,gcd->btgd", d, w.astype(jnp.float32)).reshape(B, T, BRANCH_W)
    return y * scale.astype(jnp.float32)


def linear_scan(a, b, h0, reverse):
    def combine(left, right):
        al, bl = left
        ar, br = right
        return ar * al, ar * bl + br
    A, H = lax.associative_scan(combine, (a, b), axis=1, reverse=reverse)
    return H + A * h0[:, None, :]


def rglru_mixer(zx, zg, h0, conv_w, conv_b, wa, ba, wx, bx, lam):
    B, T, W = zx.shape
    xf = zx.astype(jnp.float32)
    xp = jnp.pad(xf, ((0, 0), (LRU_CONV_LEFT, LRU_CONV - 1 - LRU_CONV_LEFT), (0, 0)))
    xc = conv_b.astype(jnp.float32)
    for j in range(LRU_CONV):
        xc = xc + xp[:, j:j + T] * conv_w[j].astype(jnp.float32)
    xb = xc.reshape(B, T, LRU_BLOCKS, LRU_BW)
    hs = []
    for d in range(2):
        r = jax.nn.sigmoid(jnp.einsum("btnc,ncd->btnd", xb, wa[d].astype(jnp.float32)).reshape(B, T, W) + ba[d])
        i = jax.nn.sigmoid(jnp.einsum("btnc,ncd->btnd", xb, wx[d].astype(jnp.float32)).reshape(B, T, W) + bx[d])
        log_a = -LRU_C * r * jax.nn.softplus(-lam[d].astype(jnp.float32))
        a = jnp.exp(log_a)
        b = jnp.sqrt(-jnp.expm1(2.0 * log_a)) * (i * xc)
        hs.append(linear_scan(a, b, h0[:, d].astype(jnp.float32), reverse=(d == 1)))
    y = (hs[0] + hs[1]) * jax.nn.gelu(zg.astype(jnp.float32))
    state = jnp.stack([hs[0][:, -1], hs[1][:, 0]], axis=1)
    return y, state


def wkv_scan(r, w, k, v, a, b, S0, reverse):
    def step(S, inp):
        r_t, w_t, k_t, v_t, a_t, b_t = inp
        sa = jnp.einsum("bhij,bhj->bhi", S, a_t)
        S = S * w_t[:, :, None, :] + sa[..., None] * b_t[:, :, None, :] + v_t[..., None] * k_t[:, :, None, :]
        return S, jnp.einsum("bhij,bhj->bhi", S, r_t)
    xs = tuple(jnp.moveaxis(t, 1, 0) for t in (r, w, k, v, a, b))
    S, ys = lax.scan(step, S0, xs, reverse=reverse)
    return jnp.moveaxis(ys, 0, 1), S


def rwkv7_mixer(z, S0, v_first, mu, w0, w2, a0, a2, k_k, k_a, r_k, g2, gn_w, gn_b, vres):
    B, T, _ = z.shape
    C = BRANCH_W
    z = z.astype(jnp.float32)
    z = z + mu * (token_shift_centred(z) - z)
    r, k, v, wd, ad, gd = jnp.split(z, RWKV_SPLITS, axis=-1)
    wd = wd.reshape(B, T, 2, RWKV_W_RANK)
    ad = ad.reshape(B, T, 2, RWKV_A_RANK)
    if vres is None:
        v_first = v
    else:
        v0, v1, v2 = vres
        v = v + (v_first - v) * jax.nn.sigmoid(v0 + (v @ v1) @ v2)
    w_log = -jax.nn.softplus(-(w0 + jnp.einsum("btdr,drc->btdc", jnp.tanh(wd), w2))) - 0.5
    decay = jnp.exp(-jnp.exp(w_log))
    a = jax.nn.sigmoid(a0 + jnp.einsum("btdr,drc->btdc", ad, a2))
    g = jax.nn.sigmoid(gd) @ g2
    hd = lambda t: t.reshape(B, T, RWKV_H, RWKV_HEAD)
    kk = hd(k * k_k)
    kk = kk / jnp.maximum(jnp.sqrt(jnp.sum(kk * kk, axis=-1, keepdims=True)), 1e-12)
    kk = kk.reshape(B, T, C)
    k_dir = k[:, :, None, :] * (1.0 + (a - 1.0) * k_a)
    S0 = S0.astype(jnp.float32)
    y_f, S_f = wkv_scan(hd(r), hd(decay[:, :, 0]), hd(k_dir[:, :, 0]), hd(v), hd(-kk),
                        hd(kk * a[:, :, 0]), S0[:, 0], False)
    y_b, S_b = wkv_scan(hd(r), hd(decay[:, :, 1]), hd(k_dir[:, :, 1]), hd(v), hd(-kk),
                        hd(kk * a[:, :, 1]), S0[:, 1], True)
    y = y_f + y_b
    mean = jnp.mean(y, axis=-1, keepdims=True)
    var = jnp.mean(jnp.square(y - mean), axis=-1, keepdims=True)
    y = ((y - mean) * lax.rsqrt(var + RWKV_GN_EPS)).reshape(B, T, C) * gn_w + gn_b
    bonus = jnp.sum(hd(r) * hd(k) * r_k, axis=-1, keepdims=True) * hd(v)
    out = (y + bonus.reshape(B, T, C)) * g
    return out, v_first, jnp.stack([S_f, S_b], axis=1)


def moe_ffn(x, router_w, router_b, w_gate, b_gate, w_up, b_up, w_down, b_down):
    shp = x.shape
    xt = x.reshape(-1, D_MODEL)
    N = xt.shape[0]
    logits = (xt @ router_w).astype(jnp.float32) + router_b.astype(jnp.float32)
    top_v, top_i = lax.top_k(logits, TOP_K)
    gates = jax.nn.softmax(top_v, axis=-1)
    flat_e = top_i.reshape(-1).astype(jnp.int32)
    flat_g = gates.reshape(-1)
    flat_t = jnp.arange(N * TOP_K, dtype=jnp.int32) // TOP_K
    order = jnp.argsort(flat_e)
    se, sg, st = flat_e[order], flat_g[order], flat_t[order]
    counts = jnp.bincount(flat_e, length=N_EXPERTS).astype(jnp.int32)
    start = jnp.cumsum(counts) - counts
    padded = (counts + MOE_BLOCK - 1) // MOE_BLOCK * MOE_BLOCK
    pend = jnp.cumsum(padded)
    pstart = pend - padded
    dest = pstart[se] + jnp.arange(N * TOP_K, dtype=jnp.int32) - start[se]
    n_blocks = -(-(N * TOP_K) // MOE_BLOCK) + N_EXPERTS
    n_rows = n_blocks * MOE_BLOCK
    buf_t = jnp.full((n_rows,), N, jnp.int32).at[dest].set(st)
    buf_g = jnp.zeros((n_rows,), jnp.float32).at[dest].set(sg)
    block_e = jnp.minimum(jnp.searchsorted(pend // MOE_BLOCK, jnp.arange(n_blocks), side="right"),
                          N_EXPERTS - 1)
    x_pad = jnp.concatenate([xt, jnp.zeros((1, D_MODEL), xt.dtype)], axis=0)
    xb = x_pad[buf_t].reshape(n_blocks, MOE_BLOCK, D_MODEL)

    def expert_block(args):
        xblk, e = args
        gate = xblk @ w_gate[e] + b_gate[e]
        up = xblk @ w_up[e] + b_up[e]
        gate = jnp.minimum(gate, SWIGLU_LIMIT)
        up = jnp.clip(up, -SWIGLU_LIMIT, SWIGLU_LIMIT)
        glu = gate * jax.nn.sigmoid(gate * SWIGLU_ALPHA)
        return ((up + 1.0) * glu) @ w_down[e] + b_down[e]

    yb = lax.map(expert_block, (xb, block_e))
    y = jnp.zeros((N + 1, D_MODEL), jnp.float32).at[buf_t].add(
        yb.reshape(n_rows, D_MODEL).astype(jnp.float32) * buf_g[:, None])
    return y[:N].astype(x.dtype).reshape(shp)


def trunk(x, cvec, S0_all, h0_all, P):
    B, T, _ = x.shape
    v_first = None
    S_list, h_list = [], []
    for l in range(DEPTH):
        mod = jax.nn.silu(cvec) @ P["ada_w"][l] + P["ada_b"][l]
        sh1, sc1, g1, sh2, sc2, g2 = jnp.split(mod.astype(x.dtype)[:, None, :], 6, axis=-1)
        nrm = P["norms"][l]
        xn = rmsnorm(x, nrm[0]) * (1.0 + sc1) + sh1
        p = xn @ P["w_in"][l]
        z_f, z_r, z_lx, z_lg, z_p, z_g = jnp.split(p, W_IN_SPLITS, axis=-1)
        u_f = fourier_mixer(z_f, P["fnet_w"][l])
        vres = None if l == 0 else (P["rwkv_v0"][l - 1], P["rwkv_v1"][l - 1], P["rwkv_v2"][l - 1])
        u_r, v_first, S_l = rwkv7_mixer(z_r, S0_all[:, l], v_first, P["rwkv_mu"][l], P["rwkv_w0"][l],
                                        P["rwkv_w2"][l], P["rwkv_a0"][l], P["rwkv_a2"][l], P["rwkv_kk"][l],
                                        P["rwkv_ka"][l], P["rwkv_rk"][l], P["rwkv_g2"][l],
                                        P["rwkv_gn_w"][l], P["rwkv_gn_b"][l], vres)
        u_l, h_l = rglru_mixer(z_lx, z_lg, h0_all[:, l], P["lru_conv_w"][l], P["lru_conv_b"][l],
                               P["lru_wa"][l], P["lru_ba"][l], P["lru_wx"][l], P["lru_bx"][l],
                               P["lru_lam"][l])
        u_p = pool_mixer(z_p, P["pool_w"][l], P["pool_scale"][l])
        u = jnp.stack([u_f, u_r, u_l, u_p], axis=2).astype(x.dtype)
        branch = jnp.einsum("btnc,ncd->btnd", u, P["w_branch"][l])
        gate = jax.nn.sigmoid(z_g.reshape(B, T, N_BRANCH, D_MODEL))
        mix = jnp.sum(gate * branch, axis=2) @ P["w_out"][l]
        x = x + g1 * rmsnorm(mix, nrm[1])
        xn2 = rmsnorm(x, nrm[2]) * (1.0 + sc2) + sh2
        ff = moe_ffn(xn2, P["router_w"][l], P["router_b"][l], P["moe_w_gate"][l], P["moe_b_gate"][l],
                     P["moe_w_up"][l], P["moe_b_up"][l], P["moe_w_down"][l], P["moe_b_down"][l])
        x = x + g2 * rmsnorm(ff, nrm[3])
        S_list.append(S_l)
        h_list.append(h_l)
    return x, jnp.stack(S_list, axis=1), jnp.stack(h_list, axis=1)


def setup_inputs(seed: int = 0) -> dict:
    key = jax.random.key(seed)
    ks = iter(jax.random.split(key, 64))

    def nrm(shape, scale):
        return jax.random.normal(next(ks), shape, jnp.float32) * scale

    def unif(shape, lo, hi):
        return jax.random.uniform(next(ks), shape, jnp.float32, lo, hi)

    L = DEPTH
    C = BRANCH_W
    a_c = unif((L, 2, LRU_W), 0.9, 0.999)
    a_base = a_c ** (1.0 / LRU_C)
    return {
        "x_prompt": nrm((BATCH, SEQ, D_MODEL), 1.0),
        "x_sample": nrm((DEC_BATCH, DEC_SEQ, D_MODEL), 1.0),
        "state_rwkv": nrm((DEC_BATCH, L, 2, RWKV_H, RWKV_HEAD, RWKV_HEAD), 0.5),
        "state_lru": nrm((DEC_BATCH, L, 2, LRU_W), 0.5),
        "c": nrm((DEC_BATCH, D_MODEL), 1.0),
        "c_ctx": nrm((D_MODEL,), 1.0),
        "ada_w": nrm((L, D_MODEL, 6 * D_MODEL), 0.5 * D_MODEL ** -0.5),
        "ada_b": nrm((L, 6 * D_MODEL), 0.02),
        "norms": 1.0 + nrm((L, 4, D_MODEL), 0.05),
        "w_in": nrm((L, D_MODEL, P_IN), D_MODEL ** -0.5),
        "fnet_w": nrm((L, FNET_GROUPS, FNET_GW, FNET_GW), FNET_GW ** -0.5),
        "rwkv_mu": unif((L, RWKV_IN), 0.0, 1.0),
        "rwkv_w0": unif((L, 2, C), -6.0, -1.0),
        "rwkv_w2": nrm((L, 2, RWKV_W_RANK, C), 0.5 * RWKV_W_RANK ** -0.5),
        "rwkv_a0": nrm((L, 2, C), 0.1),
        "rwkv_a2": nrm((L, 2, RWKV_A_RANK, C), 0.5 * RWKV_A_RANK ** -0.5),
        "rwkv_kk": 0.85 + nrm((L, C), 0.05),
        "rwkv_ka": 1.0 + nrm((L, 2, C), 0.05),
        "rwkv_rk": nrm((L, RWKV_H, RWKV_HEAD), 0.1),
        "rwkv_g2": nrm((L, RWKV_G_RANK, C), RWKV_G_RANK ** -0.5),
        "rwkv_gn_w": 1.0 + nrm((L, C), 0.05),
        "rwkv_gn_b": nrm((L, C), 0.02),
        "rwkv_v0": nrm((L - 1, C), 0.5),
        "rwkv_v1": nrm((L - 1, C, RWKV_V_RANK), C ** -0.5),
        "rwkv_v2": nrm((L - 1, RWKV_V_RANK, C), RWKV_V_RANK ** -0.5),
        "lru_conv_w": nrm((L, LRU_CONV, LRU_W), LRU_CONV ** -0.5),
        "lru_conv_b": nrm((L, LRU_W), 0.02),
        "lru_wa": nrm((L, 2, LRU_BLOCKS, LRU_BW, LRU_BW), LRU_BW ** -0.5),
        "lru_ba": nrm((L, 2, LRU_W), 0.02),
        "lru_wx": nrm((L, 2, LRU_BLOCKS, LRU_BW, LRU_BW), LRU_BW ** -0.5),
        "lru_bx": nrm((L, 2, LRU_W), 0.02),
        "lru_lam": jnp.log(a_base) - jnp.log1p(-a_base),
        "pool_w": nrm((L, len(POOL_WINDOWS), POOL_GW, POOL_GW), POOL_GW ** -0.5),
        "pool_scale": 1.0 + nrm((L, C), 0.1),
        "w_branch": nrm((L, N_BRANCH, C, D_MODEL), C ** -0.5),
        "w_out": nrm((L, D_MODEL, D_MODEL), D_MODEL ** -0.5),
        "router_w": nrm((L, D_MODEL, N_EXPERTS), D_MODEL ** -0.5),
        "router_b": nrm((L, N_EXPERTS), 0.01),
        "moe_w_gate": nrm((L, N_EXPERTS, D_MODEL, EXPERT_FF), D_MODEL ** -0.5),
        "moe_b_gate": nrm((L, N_EXPERTS, EXPERT_FF), 0.01),
        "moe_w_up": nrm((L, N_EXPERTS, D_MODEL, EXPERT_FF), D_MODEL ** -0.5),
        "moe_b_up": nrm((L, N_EXPERTS, EXPERT_FF), 0.01),
        "moe_w_down": nrm((L, N_EXPERTS, EXPERT_FF, D_MODEL), EXPERT_FF ** -0.5),
        "moe_b_down": nrm((L, N_EXPERTS, D_MODEL), 0.01),
    }


def reference(x_prompt, x_sample, state_rwkv, state_lru, c, c_ctx, ada_w, ada_b, norms, w_in, fnet_w,
              rwkv_mu, rwkv_w0, rwkv_w2, rwkv_a0, rwkv_a2, rwkv_kk, rwkv_ka, rwkv_rk, rwkv_g2,
              rwkv_gn_w, rwkv_gn_b, rwkv_v0, rwkv_v1, rwkv_v2, lru_conv_w, lru_conv_b, lru_wa, lru_ba,
              lru_wx, lru_bx, lru_lam, pool_w, pool_scale, w_branch, w_out, router_w, router_b,
              moe_w_gate, moe_b_gate, moe_w_up, moe_b_up, moe_w_down, moe_b_down):
    P = dict(ada_w=ada_w, ada_b=ada_b, norms=norms, w_in=w_in, fnet_w=fnet_w, rwkv_mu=rwkv_mu,
             rwkv_w0=rwkv_w0, rwkv_w2=rwkv_w2, rwkv_a0=rwkv_a0, rwkv_a2=rwkv_a2, rwkv_kk=rwkv_kk,
             rwkv_ka=rwkv_ka, rwkv_rk=rwkv_rk, rwkv_g2=rwkv_g2, rwkv_gn_w=rwkv_gn_w, rwkv_gn_b=rwkv_gn_b,
             rwkv_v0=rwkv_v0, rwkv_v1=rwkv_v1, rwkv_v2=rwkv_v2, lru_conv_w=lru_conv_w,
             lru_conv_b=lru_conv_b, lru_wa=lru_wa, lru_ba=lru_ba, lru_wx=lru_wx, lru_bx=lru_bx,
             lru_lam=lru_lam, pool_w=pool_w, pool_scale=pool_scale, w_branch=w_branch, w_out=w_out,
             router_w=router_w, router_b=router_b, moe_w_gate=moe_w_gate, moe_b_gate=moe_b_gate,
             moe_w_up=moe_w_up, moe_b_up=moe_b_up, moe_w_down=moe_w_down, moe_b_down=moe_b_down)
    b_ctx = x_prompt.shape[0]
    S0_ctx = jnp.zeros((b_ctx, DEPTH, 2, RWKV_H, RWKV_HEAD, RWKV_HEAD), jnp.float32)
    h0_ctx = jnp.zeros((b_ctx, DEPTH, 2, LRU_W), jnp.float32)
    y_prompt, new_rwkv, new_lru = trunk(x_prompt, c_ctx[None, :], S0_ctx, h0_ctx, P)
    x_lat = x_sample + grid_position_code(x_sample.shape[1]).astype(x_sample.dtype)[None]
    y_sample, _, _ = trunk(x_lat, c, state_rwkv, state_lru, P)
    return (y_prompt, y_sample, new_rwkv.astype(x_prompt.dtype), new_lru.astype(x_prompt.dtype))
```

```python
import functools

import jax
import jax.numpy as jnp
from jax import lax
from jax.experimental import pallas as pl
from jax.experimental.pallas import tpu as pltpu

D_MODEL = 4096
BATCH = 32
SEQ = 256
DEPTH = 2
DEC_BATCH = 2
DEC_SEQ = 1024

GRID_W = 64
N_BRANCH = 4
BRANCH_W = D_MODEL // 4
FNET_GROUPS = 4
FNET_GW = BRANCH_W // FNET_GROUPS
RWKV_HEAD = 64
RWKV_H = BRANCH_W // RWKV_HEAD
RWKV_W_RANK = 64
RWKV_A_RANK = 64
RWKV_V_RANK = 32
RWKV_G_RANK = 160
RWKV_GN_EPS = 64e-5
RWKV_IN = 3 * BRANCH_W + 2 * RWKV_W_RANK + 2 * RWKV_A_RANK + RWKV_G_RANK
RWKV_SPLITS = (BRANCH_W, 2 * BRANCH_W, 3 * BRANCH_W,
               3 * BRANCH_W + 2 * RWKV_W_RANK,
               3 * BRANCH_W + 2 * RWKV_W_RANK + 2 * RWKV_A_RANK)
LRU_W = BRANCH_W
LRU_BLOCKS = 4
LRU_BW = LRU_W // LRU_BLOCKS
LRU_CONV = 4
LRU_CONV_LEFT = 1
LRU_C = 8.0
POOL_WINDOWS = (2, 4, 8, 16)
POOL_GW = BRANCH_W // len(POOL_WINDOWS)
P_IN = BRANCH_W + RWKV_IN + 2 * LRU_W + BRANCH_W + N_BRANCH * D_MODEL
N_EXPERTS = 32
TOP_K = 4
EXPERT_FF = D_MODEL // 2
SWIGLU_LIMIT = 7.0
SWIGLU_ALPHA = 1.702
RMS_EPS = 1e-6
POS_BASE = 10000.0

LANES = 128
N_CTX = BATCH * SEQ
N_LAT = DEC_BATCH * DEC_SEQ
N_TOK = N_CTX + N_LAT
N_GROUPS = 1 + DEC_BATCH
RWKV_PAD = -(-RWKV_IN // LANES) * LANES
REST_OFF = BRANCH_W + RWKV_IN
REST_W = P_IN - REST_OFF
LP_W = 2 * LRU_W + BRANCH_W
MOE_ROWS_BLK = 256
MOE_BLOCKS = -(-(N_TOK * TOP_K) // MOE_ROWS_BLK) + N_EXPERTS
MOE_ROWS = MOE_BLOCKS * MOE_ROWS_BLK
VMEM_LIMIT = 56 * 1024 * 1024

F32 = jnp.float32
BF16 = jnp.bfloat16


def _params(*sem):
    return pltpu.CompilerParams(dimension_semantics=sem, vmem_limit_bytes=VMEM_LIMIT)


def _group_of_tile(i, tm):
    n_ctx_tiles = N_CTX // tm
    return jnp.where(i < n_ctx_tiles, 0, 1 + (i - n_ctx_tiles) // (DEC_SEQ // tm))


def _rms(x):
    return x * lax.rsqrt(jnp.mean(x * x, axis=-1, keepdims=True) + RMS_EPS)


def _ada_kernel(c_ref, w_ref, b_ref, o_ref):
    c = c_ref[...]
    s = c * jax.nn.sigmoid(c)
    o_ref[...] = jnp.dot(s.astype(BF16), w_ref[...].astype(BF16),
                         preferred_element_type=F32) + b_ref[...]


def ada_modulation(cvec, ada_w, ada_b):
    tn = 512
    return pl.pallas_call(
        _ada_kernel,
        grid=(DEPTH, 6 * D_MODEL // tn),
        in_specs=[pl.BlockSpec((8, D_MODEL), lambda l, j: (0, 0)),
                  pl.BlockSpec((None, D_MODEL, tn), lambda l, j: (l, 0, j)),
                  pl.BlockSpec((None, 1, tn), lambda l, j: (l, 0, j))],
        out_specs=pl.BlockSpec((None, 8, tn), lambda l, j: (l, 0, j)),
        out_shape=jax.ShapeDtypeStruct((DEPTH, 8, 6 * D_MODEL), F32),
        compiler_params=_params("arbitrary", "arbitrary"),
        name="ada_modulation",
    )(cvec, ada_w, ada_b.reshape(DEPTH, 1, 6 * D_MODEL))


def _norm_mod_kernel(x_ref, g_ref, sc_ref, sh_ref, o_ref):
    y = _rms(x_ref[...]) * g_ref[...]
    o_ref[...] = (y * (1.0 + sc_ref[...]) + sh_ref[...]).astype(o_ref.dtype)


def _mod_spec(l, which, tm):
    return pl.BlockSpec((None, None, None, 1, D_MODEL),
                        lambda i: (l, _group_of_tile(i, tm), which, 0, 0))


def _norm_spec(l, which):
    return pl.BlockSpec((None, None, 1, D_MODEL), lambda i: (l, which, 0, 0))


def norm_mod(x, norms, mod, l, norm_idx, sc_idx, sh_idx):
    tm = 256
    row = pl.BlockSpec((tm, D_MODEL), lambda i: (i, 0))
    return pl.pallas_call(
        _norm_mod_kernel,
        grid=(N_TOK // tm,),
        in_specs=[row, _norm_spec(l, norm_idx), _mod_spec(l, sc_idx, tm), _mod_spec(l, sh_idx, tm)],
        out_specs=row,
        out_shape=jax.ShapeDtypeStruct((N_TOK, D_MODEL), BF16),
        compiler_params=_params("arbitrary"),
        name="norm_mod",
    )(x, norms, mod, mod)


def _resid_norm_kernel(x_ref, y_ref, gy_ref, gate_ref, g2_ref, sc_ref, sh_ref, x1_ref, xn_ref):
    x1 = x_ref[...] + gate_ref[...] * (_rms(y_ref[...]) * gy_ref[...])
    x1_ref[...] = x1
    xn = _rms(x1) * g2_ref[...]
    xn_ref[...] = (xn * (1.0 + sc_ref[...]) + sh_ref[...]).astype(xn_ref.dtype)


def resid_norm(x, y, norms, mod, l, ynorm_idx, gate_idx, l2, norm2_idx, sc_idx, sh_idx):
    tm = 256
    row = pl.BlockSpec((tm, D_MODEL), lambda i: (i, 0))
    return pl.pallas_call(
        _resid_norm_kernel,
        grid=(N_TOK // tm,),
        in_specs=[row, row, _norm_spec(l, ynorm_idx), _mod_spec(l, gate_idx, tm),
                  _norm_spec(l2, norm2_idx), _mod_spec(l2, sc_idx, tm), _mod_spec(l2, sh_idx, tm)],
        out_specs=[row, row],
        out_shape=[jax.ShapeDtypeStruct((N_TOK, D_MODEL), F32),
                   jax.ShapeDtypeStruct((N_TOK, D_MODEL), BF16)],
        compiler_params=_params("arbitrary"),
        name="resid_norm",
    )(x, y, norms, mod, norms, mod, mod)


def _mm_kernel(a_ref, w_ref, o_ref):
    o_ref[...] = jnp.dot(a_ref[...], w_ref[...].astype(BF16),
                         preferred_element_type=F32).astype(o_ref.dtype)


def matmul_cols(a, w, lead, col_off, n_cols, out_dtype, name):
    m, k = a.shape
    tm, tn = 1024, 512
    assert m % tm == 0 and n_cols % tn == 0 and col_off % tn == 0
    nlead = len(lead)
    w_spec = pl.BlockSpec((None,) * nlead + (k, tn), lambda i, j: lead + (0, col_off // tn + j))
    return pl.pallas_call(
        _mm_kernel,
        grid=(m // tm, n_cols // tn),
        in_specs=[pl.BlockSpec((tm, k), lambda i, j: (i, 0)), w_spec],
        out_specs=pl.BlockSpec((tm, tn), lambda i, j: (i, j)),
        out_shape=jax.ShapeDtypeStruct((m, n_cols), out_dtype),
        compiler_params=_params("arbitrary", "arbitrary"),
        name=name,
    )(a, w)


def _gate_kernel(xn_ref, u_ref, wg_ref, wb_ref, o_ref, acc_ref):
    n = pl.program_id(2)
    s = jnp.dot(xn_ref[...], wg_ref[...].astype(BF16), preferred_element_type=F32)
    br = jnp.dot(u_ref[...], wb_ref[...].astype(BF16), preferred_element_type=F32)
    val = jax.nn.sigmoid(s) * br

    @pl.when(n == 0)
    def _():
        acc_ref[...] = val

    @pl.when(n > 0)
    def _():
        acc_ref[...] += val

    @pl.when(n == N_BRANCH - 1)
    def _():
        o_ref[...] = acc_ref[...].astype(o_ref.dtype)


def gated_branches(xn, u, w_rest, w_branch, l):
    tm, tn = 1024, 256
    return pl.pallas_call(
        _gate_kernel,
        grid=(N_TOK // tm, D_MODEL // tn, N_BRANCH),
        in_specs=[pl.BlockSpec((tm, D_MODEL), lambda i, j, n: (i, 0)),
                  pl.BlockSpec((tm, BRANCH_W), lambda i, j, n: (i, n)),
                  pl.BlockSpec((D_MODEL, tn), lambda i, j, n: (0, (LP_W + n * D_MODEL) // tn + j)),
                  pl.BlockSpec((None, None, BRANCH_W, tn), lambda i, j, n: (l, n, 0, j))],
        out_specs=pl.BlockSpec((tm, tn), lambda i, j, n: (i, j)),
        out_shape=jax.ShapeDtypeStruct((N_TOK, D_MODEL), BF16),
        scratch_shapes=[pltpu.VMEM((tm, tn), F32)],
        compiler_params=_params("arbitrary", "arbitrary", "arbitrary"),
        name="gated_branches",
    )(xn, u, w_rest, w_branch)


ROUTER_TM = 512


def _router_kernel(x_ref, w_ref, b_ref, tri_ref, idx_ref, gate_ref, rank_ref, cnt_ref, carry_ref):
    @pl.when(pl.program_id(0) == 0)
    def _():
        carry_ref[...] = jnp.zeros_like(carry_ref)

    logits = jnp.dot(x_ref[...], w_ref[...], preferred_element_type=F32) + b_ref[...]
    lane = lax.broadcasted_iota(jnp.int32, logits.shape, 1).astype(F32)
    cur = jnp.where(lane < N_EXPERTS, logits, -jnp.inf)
    vals, idxs, hots = [], [], []
    for _ in range(TOP_K):
        m = jnp.max(cur, axis=-1, keepdims=True)
        idx = jnp.min(jnp.where(cur == m, lane, float(LANES)), axis=-1, keepdims=True)
        hot = lane == idx
        cur = jnp.where(hot, -jnp.inf, cur)
        vals.append(m)
        idxs.append(idx)
        hots.append(hot)
    exps = [jnp.exp(v - vals[0]) for v in vals]
    denom = exps[0] + exps[1] + exps[2] + exps[3]
    sel = jnp.zeros(logits.shape, F32)
    for hot in hots:
        sel = sel + hot.astype(F32)
    before = jnp.dot(tri_ref[...], sel.astype(BF16), preferred_element_type=F32) + carry_ref[...]
    idx_out = jnp.zeros(logits.shape, F32)
    gate_out = jnp.zeros(logits.shape, F32)
    rank_out = jnp.zeros(logits.shape, F32)
    for k in range(TOP_K):
        rank = jnp.sum(jnp.where(hots[k], before, 0.0), axis=-1, keepdims=True)
        idx_out = jnp.where(lane == k, idxs[k], idx_out)
        gate_out = jnp.where(lane == k, exps[k] / denom, gate_out)
        rank_out = jnp.where(lane == k, rank, rank_out)
    idx_ref[...] = idx_out.astype(jnp.int32)
    gate_ref[...] = gate_out
    rank_ref[...] = rank_out.astype(jnp.int32)
    carry_ref[...] += jnp.sum(sel, axis=0, keepdims=True)
    cnt_ref[...] = carry_ref[...].astype(jnp.int32)


def moe_route(xn2, router_w, router_b, l):
    tm = ROUTER_TM
    w = jnp.zeros((D_MODEL, LANES), BF16).at[:, :N_EXPERTS].set(router_w[l].astype(BF16))
    b = jnp.zeros((1, LANES), F32).at[0, :N_EXPERTS].set(router_b[l])
    tri = (lax.broadcasted_iota(jnp.int32, (tm, tm), 1)
           < lax.broadcasted_iota(jnp.int32, (tm, tm), 0)).astype(BF16)
    row = pl.BlockSpec((tm, LANES), lambda i: (i, 0))
    idx, gates, rank, cnt = pl.pallas_call(
        _router_kernel,
        grid=(N_TOK // tm,),
        in_specs=[pl.BlockSpec((tm, D_MODEL), lambda i: (i, 0)),
                  pl.BlockSpec((D_MODEL, LANES), lambda i: (0, 0)),
                  pl.BlockSpec((1, LANES), lambda i: (0, 0)),
                  pl.BlockSpec((tm, tm), lambda i: (0, 0))],
        out_specs=[row, row, row, pl.BlockSpec((1, LANES), lambda i: (0, 0))],
        out_shape=[jax.ShapeDtypeStruct((N_TOK, LANES), jnp.int32),
                   jax.ShapeDtypeStruct((N_TOK, LANES), F32),
                   jax.ShapeDtypeStruct((N_TOK, LANES), jnp.int32),
                   jax.ShapeDtypeStruct((1, LANES), jnp.int32)],
        scratch_shapes=[pltpu.VMEM((1, LANES), F32)],
        compiler_params=_params("arbitrary"),
        name="moe_route",
    )(xn2, w, b, tri)
    return idx[:, :TOP_K], gates[:, :TOP_K], rank[:, :TOP_K], cnt[0, :N_EXPERTS]


def _row_copy(src_hbm, dst_vmem, src_row, dst_row, sem):
    return pltpu.make_async_copy(src_hbm.at[pl.ds(src_row, 1), :], dst_vmem.at[pl.ds(dst_row, 1), :], sem)


def _gather_rows_kernel(tok_ref, x_hbm, o_ref, sem):
    base = pl.program_id(0) * MOE_ROWS_BLK

    def start(r, c):
        _row_copy(x_hbm, o_ref, tok_ref[base + r], r, sem).start()
        return c

    def wait(r, c):
        _row_copy(x_hbm, o_ref, 0, r, sem).wait()
        return c

    lax.fori_loop(0, MOE_ROWS_BLK, start, 0)
    lax.fori_loop(0, MOE_ROWS_BLK, wait, 0)


def gather_rows(row_token, x_words):
    width = x_words.shape[1]
    return pl.pallas_call(
        _gather_rows_kernel,
        grid_spec=pltpu.PrefetchScalarGridSpec(
            num_scalar_prefetch=1,
            grid=(MOE_BLOCKS,),
            in_specs=[pl.BlockSpec(memory_space=pl.ANY)],
            out_specs=pl.BlockSpec((MOE_ROWS_BLK, width), lambda i, tok: (i, 0)),
            scratch_shapes=[pltpu.SemaphoreType.DMA(())]),
        out_shape=jax.ShapeDtypeStruct((MOE_ROWS, width), x_words.dtype),
        compiler_params=_params("arbitrary"),
        name="moe_gather_rows",
    )(row_token, x_words)


UP_TF = 512
DOWN_TN = 1024


def _expert_up_kernel(ie_ref, if_ref, ii_ref, first_ref, valid_ref,
                      x_ref, wg_ref, wu_ref, bg_ref, bu_ref, h_ref, wg_bf, wu_bf):
    q = pl.program_id(0)

    @pl.when(first_ref[q] == 1)
    def _():
        wg_bf[...] = wg_ref[...].astype(BF16)
        wu_bf[...] = wu_ref[...].astype(BF16)

    @pl.when(valid_ref[q] == 1)
    def _():
        x = x_ref[...]
        gate = jnp.dot(x, wg_bf[...], preferred_element_type=F32) + bg_ref[...]
        up = jnp.dot(x, wu_bf[...], preferred_element_type=F32) + bu_ref[...]
        gate = jnp.minimum(gate, SWIGLU_LIMIT)
        up = jnp.clip(up, -SWIGLU_LIMIT, SWIGLU_LIMIT)
        glu = gate * jax.nn.sigmoid(gate * SWIGLU_ALPHA)
        h_ref[...] = ((up + 1.0) * glu).astype(h_ref.dtype)

    @pl.when(valid_ref[q] == 0)
    def _():
        h_ref[...] = jnp.zeros_like(h_ref)


def expert_up(items, xs, w_gate, b_gate, w_up, b_up, l):
    tf = UP_TF
    n_items = items[0].shape[0]
    w_spec = pl.BlockSpec((None, None, D_MODEL, tf), lambda q, ie, jf, ii, fi, va: (l, ie[q], 0, jf[q]))
    b_spec = pl.BlockSpec((None, None, 1, tf), lambda q, ie, jf, ii, fi, va: (l, ie[q], 0, jf[q]))
    return pl.pallas_call(
        _expert_up_kernel,
        grid_spec=pltpu.PrefetchScalarGridSpec(
            num_scalar_prefetch=5,
            grid=(n_items,),
            in_specs=[pl.BlockSpec((MOE_ROWS_BLK, D_MODEL), lambda q, ie, jf, ii, fi, va: (ii[q], 0)),
                      w_spec, w_spec, b_spec, b_spec],
            out_specs=pl.BlockSpec((MOE_ROWS_BLK, tf), lambda q, ie, jf, ii, fi, va: (ii[q], jf[q])),
            scratch_shapes=[pltpu.VMEM((D_MODEL, tf), BF16), pltpu.VMEM((D_MODEL, tf), BF16)]),
        out_shape=jax.ShapeDtypeStruct((MOE_ROWS, EXPERT_FF), BF16),
        compiler_params=_params("arbitrary"),
        name="moe_expert_up",
    )(*items, xs, w_gate, w_up,
      b_gate.reshape(DEPTH, N_EXPERTS, 1, EXPERT_FF), b_up.reshape(DEPTH, N_EXPERTS, 1, EXPERT_FF))


def _expert_down_kernel(ie_ref, if_ref, ii_ref, first_ref, valid_ref, h_ref, wd_ref, bd_ref, y_ref, wd_bf):
    q = pl.program_id(0)

    @pl.when(first_ref[q] == 1)
    def _():
        wd_bf[...] = wd_ref[...].astype(BF16)

    @pl.when(valid_ref[q] == 1)
    def _():
        y_ref[...] = jnp.dot(h_ref[...], wd_bf[...], preferred_element_type=F32) + bd_ref[...]

    @pl.when(valid_ref[q] == 0)
    def _():
        y_ref[...] = jnp.zeros_like(y_ref)


def expert_down(items, h, w_down, b_down, l):
    tn = DOWN_TN
    n_items = items[0].shape[0]
    return pl.pallas_call(
        _expert_down_kernel,
        grid_spec=pltpu.PrefetchScalarGridSpec(
            num_scalar_prefetch=5,
            grid=(n_items,),
            in_specs=[pl.BlockSpec((MOE_ROWS_BLK, EXPERT_FF), lambda q, ie, jf, ii, fi, va: (ii[q], 0)),
                      pl.BlockSpec((None, None, EXPERT_FF, tn),
                                   lambda q, ie, jf, ii, fi, va: (l, ie[q], 0, jf[q])),
                      pl.BlockSpec((None, None, 1, tn), lambda q, ie, jf, ii, fi, va: (l, ie[q], 0, jf[q]))],
            out_specs=pl.BlockSpec((MOE_ROWS_BLK, tn), lambda q, ie, jf, ii, fi, va: (ii[q], jf[q])),
            scratch_shapes=[pltpu.VMEM((EXPERT_FF, tn), BF16)]),
        out_shape=jax.ShapeDtypeStruct((MOE_ROWS, D_MODEL), F32),
        compiler_params=_params("arbitrary"),
        name="moe_expert_down",
    )(*items, h, w_down, b_down.reshape(DEPTH, N_EXPERTS, 1, D_MODEL))


COMBINE_TM = 128


def _combine_kernel(pos_ref, y_hbm, g_ref, x_ref, gy_ref, gate_ref, g2_ref, sc_ref, sh_ref,
                    x1_ref, xn_ref, buf, sem):
    base = pl.program_id(0) * (COMBINE_TM * TOP_K)

    def start(r, c):
        for k in range(TOP_K):
            _row_copy(y_hbm, buf.at[k], pos_ref[base + r * TOP_K + k], r, sem).start()
        return c

    def wait(r, c):
        for k in range(TOP_K):
            _row_copy(y_hbm, buf.at[k], 0, r, sem).wait()
        return c

    lax.fori_loop(0, COMBINE_TM, start, 0)
    lax.fori_loop(0, COMBINE_TM, wait, 0)
    g = g_ref[...]
    ff = buf[0] * g[:, 0:1]
    for k in range(1, TOP_K):
        ff = ff + buf[k] * g[:, k:k + 1]
    x1 = x_ref[...] + gate_ref[...] * (_rms(ff) * gy_ref[...])
    x1_ref[...] = x1
    xn = _rms(x1) * g2_ref[...]
    xn_ref[...] = (xn * (1.0 + sc_ref[...]) + sh_ref[...]).astype(xn_ref.dtype)


def moe_combine(pos_flat, yb, gates, x, norms, mod, l, l2):
    tm = COMBINE_TM
    row = pl.BlockSpec((tm, D_MODEL), lambda i, p: (i, 0))

    def mod_spec(ll, which):
        return pl.BlockSpec((None, None, None, 1, D_MODEL),
                            lambda i, p: (ll, _group_of_tile(i, tm), which, 0, 0))

    def norm_spec(ll, which):
        return pl.BlockSpec((None, None, 1, D_MODEL), lambda i, p: (ll, which, 0, 0))

    return pl.pallas_call(
        _combine_kernel,
        grid_spec=pltpu.PrefetchScalarGridSpec(
            num_scalar_prefetch=1,
            grid=(N_TOK // tm,),
            in_specs=[pl.BlockSpec(memory_space=pl.ANY),
                      pl.BlockSpec((tm, TOP_K), lambda i, p: (i, 0)),
                      row, norm_spec(l, 3), mod_spec(l, 5), norm_spec(l2, 0), mod_spec(l2, 1), mod_spec(l2, 0)],
            out_specs=[row, row],
            scratch_shapes=[pltpu.VMEM((TOP_K, tm, D_MODEL), F32), pltpu.SemaphoreType.DMA(())]),
        out_shape=[jax.ShapeDtypeStruct((N_TOK, D_MODEL), F32),
                   jax.ShapeDtypeStruct((N_TOK, D_MODEL), BF16)],
        compiler_params=_params("arbitrary"),
        name="moe_combine",
    )(pos_flat, yb, gates, x, norms, mod, norms, mod, mod)


def _work_items(n_blk_e, n_tiles):
    total = MOE_BLOCKS * n_tiles
    blk_end = jnp.cumsum(n_blk_e)
    blk_start = blk_end - n_blk_e
    n_used = blk_end[-1]
    used = n_used * n_tiles
    q = jnp.arange(total, dtype=jnp.int32)
    valid = q < used
    qc = jnp.minimum(q, used - 1)
    e = jnp.searchsorted(blk_end * n_tiles, qc, side="right").astype(jnp.int32)
    rel = qc - blk_start[e] * n_tiles
    nb = jnp.maximum(n_blk_e[e], 1)
    n_tail = jnp.maximum(MOE_BLOCKS - n_used, 1)
    rel_tail = q - used
    tile = jnp.where(valid, rel // nb, rel_tail // n_tail)
    blk = jnp.where(valid, blk_start[e] + rel % nb, n_used + rel_tail % n_tail)
    first = (rel % nb == 0) & valid
    return (e, tile.astype(jnp.int32), blk.astype(jnp.int32), first.astype(jnp.int32),
            valid.astype(jnp.int32))


def moe_ffn(xn2, x1, norms, mod, l, l2, router_w, router_b, w_gate, b_gate, w_up, b_up, w_down, b_down):
    idx, gates, rank, counts = moe_route(xn2, router_w, router_b, l)
    n_blk_e = (counts + MOE_ROWS_BLK - 1) // MOE_ROWS_BLK
    pstart = (jnp.cumsum(n_blk_e) - n_blk_e) * MOE_ROWS_BLK
    pos = pstart[idx] + rank
    tok = jnp.broadcast_to(jnp.arange(N_TOK, dtype=jnp.int32)[:, None], pos.shape)
    row_token = jnp.zeros((MOE_ROWS,), jnp.int32).at[pos.reshape(-1)].set(tok.reshape(-1))
    x_words = lax.bitcast_convert_type(xn2.reshape(N_TOK, D_MODEL // 2, 2), jnp.uint32)
    xs_words = gather_rows(row_token, x_words)
    xs = lax.bitcast_convert_type(xs_words, BF16).reshape(MOE_ROWS, D_MODEL)
    h = expert_up(_work_items(n_blk_e, EXPERT_FF // UP_TF), xs, w_gate, b_gate, w_up, b_up, l)
    yb = expert_down(_work_items(n_blk_e, D_MODEL // DOWN_TN), h, w_down, b_down, l)
    return moe_combine(pos.reshape(-1).astype(jnp.int32), yb, gates, x1, norms, mod, l, l2)


def _token_shift_centred(z):
    zp = jnp.pad(z, ((0, 0), (1, 1), (0, 0)))
    return 0.5 * (zp[:, :-2] + zp[:, 2:])


def _fourier_mixer(z, w):
    B, T, _ = z.shape
    zg = z.astype(F32).reshape(B, T, FNET_GROUPS, FNET_GW)
    f = jnp.fft.fft2(zg, axes=(1, 3), norm="ortho").real
    y = jnp.einsum("btgc,gcd->btgd", f, w.astype(F32))
    return y.reshape(B, T, BRANCH_W)


def _pool_mixer(z, w, scale):
    B, T, _ = z.shape
    n_g = len(POOL_WINDOWS)
    zg = z.astype(F32).reshape(B, T, n_g, POOL_GW)
    cs = jnp.concatenate([jnp.zeros((B, 1, n_g, POOL_GW), F32), jnp.cumsum(zg, axis=1)], axis=1)
    t = jnp.arange(T)
    pooled = []
    for gi, win in enumerate(POOL_WINDOWS):
        lo = jnp.clip(t - win // 2, 0, T)
        hi = jnp.clip(t + win // 2, 0, T)
        csg = cs[:, :, gi]
        cnt = (hi - lo).astype(F32)
        pooled.append((csg[:, hi] - csg[:, lo]) / cnt[None, :, None])
    d = jnp.stack(pooled, axis=2) - zg
    y = jnp.einsum("btgc,gcd->btgd", d, w.astype(F32)).reshape(B, T, BRANCH_W)
    return y * scale.astype(F32)


def _linear_scan(a, b, h0, reverse):
    def combine(left, right):
        al, bl = left
        ar, br = right
        return ar * al, ar * bl + br
    A, H = lax.associative_scan(combine, (a, b), axis=1, reverse=reverse)
    return H + A * h0[:, None, :]


def _rglru_mixer(zx, zg, h0, conv_w, conv_b, wa, ba, wx, bx, lam):
    B, T, W = zx.shape
    xf = zx.astype(F32)
    xp = jnp.pad(xf, ((0, 0), (LRU_CONV_LEFT, LRU_CONV - 1 - LRU_CONV_LEFT), (0, 0)))
    xc = conv_b.astype(F32)
    for j in range(LRU_CONV):
        xc = xc + xp[:, j:j + T] * conv_w[j].astype(F32)
    xb = xc.reshape(B, T, LRU_BLOCKS, LRU_BW)
    hs = []
    for d in range(2):
        r = jax.nn.sigmoid(jnp.einsum("btnc,ncd->btnd", xb, wa[d].astype(F32)).reshape(B, T, W) + ba[d])
        i = jax.nn.sigmoid(jnp.einsum("btnc,ncd->btnd", xb, wx[d].astype(F32)).reshape(B, T, W) + bx[d])
        log_a = -LRU_C * r * jax.nn.softplus(-lam[d].astype(F32))
        a = jnp.exp(log_a)
        b = jnp.sqrt(-jnp.expm1(2.0 * log_a)) * (i * xc)
        hs.append(_linear_scan(a, b, h0[:, d].astype(F32), reverse=(d == 1)))
    y = (hs[0] + hs[1]) * jax.nn.gelu(zg.astype(F32))
    state = jnp.stack([hs[0][:, -1], hs[1][:, 0]], axis=1)
    return y, state


def _wkv_scan(r, w, k, v, a, b, S0, reverse):
    def step(S, inp):
        r_t, w_t, k_t, v_t, a_t, b_t = inp
        sa = jnp.einsum("bhij,bhj->bhi", S, a_t)
        S = S * w_t[:, :, None, :] + sa[..., None] * b_t[:, :, None, :] + v_t[..., None] * k_t[:, :, None, :]
        return S, jnp.einsum("bhij,bhj->bhi", S, r_t)
    xs = tuple(jnp.moveaxis(t, 1, 0) for t in (r, w, k, v, a, b))
    S, ys = lax.scan(step, S0, xs, reverse=reverse)
    return jnp.moveaxis(ys, 0, 1), S


def _rwkv7_mixer(z, S0, v_first, mu, w0, w2, a0, a2, k_k, k_a, r_k, g2, gn_w, gn_b, vres):
    B, T, _ = z.shape
    C = BRANCH_W
    z = z.astype(F32)
    z = z + mu * (_token_shift_centred(z) - z)
    r, k, v, wd, ad, gd = jnp.split(z, RWKV_SPLITS, axis=-1)
    wd = wd.reshape(B, T, 2, RWKV_W_RANK)
    ad = ad.reshape(B, T, 2, RWKV_A_RANK)
    if vres is None:
        v_first = v
    else:
        v0, v1, v2 = vres
        v = v + (v_first - v) * jax.nn.sigmoid(v0 + (v @ v1) @ v2)
    w_log = -jax.nn.softplus(-(w0 + jnp.einsum("btdr,drc->btdc", jnp.tanh(wd), w2))) - 0.5
    decay = jnp.exp(-jnp.exp(w_log))
    a = jax.nn.sigmoid(a0 + jnp.einsum("btdr,drc->btdc", ad, a2))
    g = jax.nn.sigmoid(gd) @ g2
    hd = lambda t: t.reshape(B, T, RWKV_H, RWKV_HEAD)
    kk = hd(k * k_k)
    kk = kk / jnp.maximum(jnp.sqrt(jnp.sum(kk * kk, axis=-1, keepdims=True)), 1e-12)
    kk = kk.reshape(B, T, C)
    k_dir = k[:, :, None, :] * (1.0 + (a - 1.0) * k_a)
    S0 = S0.astype(F32)
    y_f, S_f = _wkv_scan(hd(r), hd(decay[:, :, 0]), hd(k_dir[:, :, 0]), hd(v), hd(-kk),
                         hd(kk * a[:, :, 0]), S0[:, 0], False)
    y_b, S_b = _wkv_scan(hd(r), hd(decay[:, :, 1]), hd(k_dir[:, :, 1]), hd(v), hd(-kk),
                         hd(kk * a[:, :, 1]), S0[:, 1], True)
    y = y_f + y_b
    mean = jnp.mean(y, axis=-1, keepdims=True)
    var = jnp.mean(jnp.square(y - mean), axis=-1, keepdims=True)
    y = ((y - mean) * lax.rsqrt(var + RWKV_GN_EPS)).reshape(B, T, C) * gn_w + gn_b
    bonus = jnp.sum(hd(r) * hd(k) * r_k, axis=-1, keepdims=True) * hd(v)
    out = (y + bonus.reshape(B, T, C)) * g
    return out, v_first, jnp.stack([S_f, S_b], axis=1)


def _grid_position_code(n_tokens):
    rows = n_tokens // GRID_W
    rr, cc = jnp.meshgrid(jnp.arange(rows, dtype=F32), jnp.arange(GRID_W, dtype=F32), indexing="ij")
    quarter = D_MODEL // 4
    omega = POS_BASE ** (-jnp.arange(quarter, dtype=F32) / quarter)

    def enc(pos):
        ang = pos.reshape(-1)[:, None] * omega[None, :]
        return jnp.concatenate([jnp.sin(ang), jnp.cos(ang)], axis=-1)

    return jnp.concatenate([enc(rr), enc(cc)], axis=-1)


def kernel(x_prompt, x_sample, state_rwkv, state_lru, c, c_ctx, ada_w, ada_b, norms, w_in, fnet_w, rwkv_mu, rwkv_w0, rwkv_w2, rwkv_a0, rwkv_a2, rwkv_kk, rwkv_ka, rwkv_rk, rwkv_g2, rwkv_gn_w, rwkv_gn_b, rwkv_v0, rwkv_v1, rwkv_v2, lru_conv_w, lru_conv_b, lru_wa, lru_ba, lru_wx, lru_bx, lru_lam, pool_w, pool_scale, w_branch, w_out, router_w, router_b, moe_w_gate, moe_b_gate, moe_w_up, moe_b_up, moe_w_down, moe_b_down):
    x_lat = x_sample + _grid_position_code(DEC_SEQ).astype(x_sample.dtype)[None]
    x = jnp.concatenate([x_prompt.reshape(N_CTX, D_MODEL), x_lat.reshape(N_LAT, D_MODEL)], axis=0)

    cvec = jnp.zeros((8, D_MODEL), F32).at[0].set(c_ctx).at[1:1 + DEC_BATCH].set(c)
    mod = ada_modulation(cvec, ada_w, ada_b)[:, :N_GROUPS].reshape(DEPTH, N_GROUPS, 6, 1, D_MODEL)
    norms4 = norms.reshape(DEPTH, 4, 1, D_MODEL)

    groups = ((0, N_CTX, BATCH, SEQ), (N_CTX, N_TOK, DEC_BATCH, DEC_SEQ))
    S0 = (jnp.zeros((BATCH, DEPTH, 2, RWKV_H, RWKV_HEAD, RWKV_HEAD), F32), state_rwkv)
    h0 = (jnp.zeros((BATCH, DEPTH, 2, LRU_W), F32), state_lru)
    v_first = [None, None]
    S_ctx, h_ctx = [], []

    xn = norm_mod(x, norms4, mod, 0, 0, 1, 0)
    for l in range(DEPTH):
        z_f = matmul_cols(xn, w_in, (l,), 0, BRANCH_W, F32, "in_proj_fourier")
        z_r = matmul_cols(xn, w_in, (l,), BRANCH_W, RWKV_PAD, F32, "in_proj_rwkv")
        w_rest = w_in[l, :, REST_OFF:]
        z_lp = matmul_cols(xn, w_rest, (), 0, LP_W, F32, "in_proj_lru_pool")
        us = []
        for gi, (lo, hi, nb, nt) in enumerate(groups):
            seq = lambda t: t[lo:hi].reshape(nb, nt, t.shape[-1])
            u_f = _fourier_mixer(seq(z_f), fnet_w[l])
            vres = None if l == 0 else (rwkv_v0[l - 1], rwkv_v1[l - 1], rwkv_v2[l - 1])
            u_r, v_first[gi], S_l = _rwkv7_mixer(
                seq(z_r)[..., :RWKV_IN], S0[gi][:, l], v_first[gi], rwkv_mu[l], rwkv_w0[l], rwkv_w2[l],
                rwkv_a0[l], rwkv_a2[l], rwkv_kk[l], rwkv_ka[l], rwkv_rk[l], rwkv_g2[l], rwkv_gn_w[l],
                rwkv_gn_b[l], vres)
            zl = seq(z_lp)
            u_l, h_l = _rglru_mixer(zl[..., :LRU_W], zl[..., LRU_W:2 * LRU_W], h0[gi][:, l], lru_conv_w[l],
                                    lru_conv_b[l], lru_wa[l], lru_ba[l], lru_wx[l], lru_bx[l], lru_lam[l])
            u_p = _pool_mixer(zl[..., 2 * LRU_W:], pool_w[l], pool_scale[l])
            us.append(jnp.concatenate([u_f, u_r, u_l, u_p], axis=-1).reshape(hi - lo, D_MODEL))
            if gi == 0:
                S_ctx.append(S_l)
                h_ctx.append(h_l)
        u = jnp.concatenate(us, axis=0).astype(BF16)
        gated = gated_branches(xn, u, w_rest, w_branch, l)
        mix = matmul_cols(gated, w_out, (l,), 0, D_MODEL, F32, "out_proj")
        x1, xn2 = resid_norm(x, mix, norms4, mod, l, 1, 2, l, 2, 4, 3)
        l2 = min(l + 1, DEPTH - 1)
        x, xn = moe_ffn(xn2, x1, norms4, mod, l, l2, router_w, router_b, moe_w_gate, moe_b_gate,
                        moe_w_up, moe_b_up, moe_w_down, moe_b_down)

    y_prompt = x[:N_CTX].reshape(BATCH, SEQ, D_MODEL)
    y_sample = x[N_CTX:].reshape(DEC_BATCH, DEC_SEQ, D_MODEL)
    return (y_prompt, y_sample, jnp.stack(S_ctx, axis=1), jnp.stack(h_ctx, axis=1))
```

```python
import functools

import jax
import jax.numpy as jnp
from jax import lax
from jax.experimental import pallas as pl
from jax.experimental.pallas import tpu as pltpu

D_MODEL = 4096
BATCH = 32
SEQ = 256
DEPTH = 2
DEC_BATCH = 2
DEC_SEQ = 1024

GRID_W = 64
N_BRANCH = 4
BRANCH_W = D_MODEL // 4
FNET_GROUPS = 4
FNET_GW = BRANCH_W // FNET_GROUPS
RWKV_HEAD = 64
RWKV_H = BRANCH_W // RWKV_HEAD
RWKV_W_RANK = 64
RWKV_A_RANK = 64
RWKV_V_RANK = 32
RWKV_G_RANK = 160
RWKV_GN_EPS = 64e-5
RWKV_IN = 3 * BRANCH_W + 2 * RWKV_W_RANK + 2 * RWKV_A_RANK + RWKV_G_RANK
LRU_W = BRANCH_W
LRU_BLOCKS = 4
LRU_BW = LRU_W // LRU_BLOCKS
LRU_CONV = 4
LRU_C = 8.0
POOL_WINDOWS = (2, 4, 8, 16)
POOL_GW = BRANCH_W // len(POOL_WINDOWS)
P_IN = BRANCH_W + RWKV_IN + 2 * LRU_W + BRANCH_W + N_BRANCH * D_MODEL
N_EXPERTS = 32
TOP_K = 4
EXPERT_FF = D_MODEL // 2
SWIGLU_LIMIT = 7.0
SWIGLU_ALPHA = 1.702
RMS_EPS = 1e-6
POS_BASE = 10000.0

LANES = 128
N_CTX = BATCH * SEQ
N_LAT = DEC_BATCH * DEC_SEQ
N_TOK = N_CTX + N_LAT
N_GROUPS = 1 + DEC_BATCH
RWKV_PAD = -(-RWKV_IN // LANES) * LANES
REST_OFF = BRANCH_W + RWKV_IN
LP_W = 2 * LRU_W + BRANCH_W
MOE_ROWS_BLK = 256
MOE_BLOCKS = -(-(N_TOK * TOP_K) // MOE_ROWS_BLK) + N_EXPERTS
MOE_ROWS = MOE_BLOCKS * MOE_ROWS_BLK
VMEM_LIMIT = 56 * 1024 * 1024

F32 = jnp.float32
BF16 = jnp.bfloat16


def _params(*sem):
    return pltpu.CompilerParams(dimension_semantics=sem, vmem_limit_bytes=VMEM_LIMIT)


def _group_of_tile(i, tm):
    n_ctx_tiles = N_CTX // tm
    return jnp.where(i < n_ctx_tiles, 0, 1 + (i - n_ctx_tiles) // (DEC_SEQ // tm))


def _rms(x):
    return x * lax.rsqrt(jnp.mean(x * x, axis=-1, keepdims=True) + RMS_EPS)


def _dot(a, b):
    return jnp.dot(a.astype(BF16), b.astype(BF16), preferred_element_type=F32)


def _dot_nt(a, b):
    return lax.dot_general(a.astype(BF16), b.astype(BF16), (((1,), (1,)), ((), ())),
                           preferred_element_type=F32)


def _split_bf16(x):
    hi = x.astype(BF16)
    return hi, (x - hi.astype(F32)).astype(BF16)


def _dot_left_exact(m, x):
    hi, lo = _split_bf16(x)
    return jnp.dot(m, hi, preferred_element_type=F32) + jnp.dot(m, lo, preferred_element_type=F32)


def _softplus(x):
    return jnp.maximum(x, 0.0) + jnp.log(1.0 + jnp.exp(-jnp.abs(x)))


def _ada_kernel(c_ref, w_ref, b_ref, o_ref):
    c = c_ref[...]
    o_ref[...] = _dot(c * jax.nn.sigmoid(c), w_ref[...]) + b_ref[...]


def ada_modulation(cvec, ada_w, ada_b):
    tn = 512
    return pl.pallas_call(
        _ada_kernel,
        grid=(DEPTH, 6 * D_MODEL // tn),
        in_specs=[pl.BlockSpec((8, D_MODEL), lambda l, j: (0, 0)),
                  pl.BlockSpec((None, D_MODEL, tn), lambda l, j: (l, 0, j)),
                  pl.BlockSpec((None, 1, tn), lambda l, j: (l, 0, j))],
        out_specs=pl.BlockSpec((None, 8, tn), lambda l, j: (l, 0, j)),
        out_shape=jax.ShapeDtypeStruct((DEPTH, 8, 6 * D_MODEL), F32),
        compiler_params=_params("arbitrary", "arbitrary"),
        name="ada_modulation",
    )(cvec, ada_w, ada_b.reshape(DEPTH, 1, 6 * D_MODEL))


def _norm_mod_kernel(x_ref, g_ref, sc_ref, sh_ref, o_ref):
    y = _rms(x_ref[...]) * g_ref[...]
    o_ref[...] = (y * (1.0 + sc_ref[...]) + sh_ref[...]).astype(o_ref.dtype)


def _mod_spec(l, which, tm):
    return pl.BlockSpec((None, None, None, 1, D_MODEL),
                        lambda i: (l, _group_of_tile(i, tm), which, 0, 0))


def _norm_spec(l, which):
    return pl.BlockSpec((None, None, 1, D_MODEL), lambda i: (l, which, 0, 0))


def norm_mod(x, norms, mod, l, norm_idx, sc_idx, sh_idx):
    tm = 256
    row = pl.BlockSpec((tm, D_MODEL), lambda i: (i, 0))
    return pl.pallas_call(
        _norm_mod_kernel,
        grid=(N_TOK // tm,),
        in_specs=[row, _norm_spec(l, norm_idx), _mod_spec(l, sc_idx, tm), _mod_spec(l, sh_idx, tm)],
        out_specs=row,
        out_shape=jax.ShapeDtypeStruct((N_TOK, D_MODEL), BF16),
        compiler_params=_params("arbitrary"),
        name="norm_mod",
    )(x, norms, mod, mod)


def _resid_norm_kernel(x_ref, y_ref, gy_ref, gate_ref, g2_ref, sc_ref, sh_ref, x1_ref, xn_ref):
    x1 = x_ref[...] + gate_ref[...] * (_rms(y_ref[...]) * gy_ref[...])
    x1_ref[...] = x1
    xn = _rms(x1) * g2_ref[...]
    xn_ref[...] = (xn * (1.0 + sc_ref[...]) + sh_ref[...]).astype(xn_ref.dtype)


def resid_norm(x, y, norms, mod, l, ynorm_idx, gate_idx, l2, norm2_idx, sc_idx, sh_idx):
    tm = 256
    row = pl.BlockSpec((tm, D_MODEL), lambda i: (i, 0))
    return pl.pallas_call(
        _resid_norm_kernel,
        grid=(N_TOK // tm,),
        in_specs=[row, row, _norm_spec(l, ynorm_idx), _mod_spec(l, gate_idx, tm),
                  _norm_spec(l2, norm2_idx), _mod_spec(l2, sc_idx, tm), _mod_spec(l2, sh_idx, tm)],
        out_specs=[row, row],
        out_shape=[jax.ShapeDtypeStruct((N_TOK, D_MODEL), F32),
                   jax.ShapeDtypeStruct((N_TOK, D_MODEL), F32)],
        compiler_params=_params("arbitrary"),
        name="resid_norm",
    )(x, y, norms, mod, norms, mod, mod)


def _mm_kernel(a_ref, w_ref, o_ref):
    o_ref[...] = _dot(a_ref[...], w_ref[...]).astype(o_ref.dtype)


def matmul_cols(a, w, lead, col_off, n_cols, out_dtype, name):
    m, k = a.shape
    tm, tn = 1024, 512
    assert m % tm == 0 and n_cols % tn == 0 and col_off % tn == 0
    nlead = len(lead)
    w_spec = pl.BlockSpec((None,) * nlead + (k, tn), lambda i, j: lead + (0, col_off // tn + j))
    return pl.pallas_call(
        _mm_kernel,
        grid=(m // tm, n_cols // tn),
        in_specs=[pl.BlockSpec((tm, k), lambda i, j: (i, 0)), w_spec],
        out_specs=pl.BlockSpec((tm, tn), lambda i, j: (i, j)),
        out_shape=jax.ShapeDtypeStruct((m, n_cols), out_dtype),
        compiler_params=_params("arbitrary", "arbitrary"),
        name=name,
    )(a, w)


def _gate_kernel(xn_ref, u_ref, wg_ref, wb_ref, o_ref, acc_ref):
    n = pl.program_id(2)
    val = jax.nn.sigmoid(_dot(xn_ref[...], wg_ref[...])) * _dot(u_ref[...], wb_ref[...])

    @pl.when(n == 0)
    def _():
        acc_ref[...] = val

    @pl.when(n > 0)
    def _():
        acc_ref[...] += val

    @pl.when(n == N_BRANCH - 1)
    def _():
        o_ref[...] = acc_ref[...].astype(o_ref.dtype)


def gated_branches(xn, u, w_rest, w_branch, l):
    tm, tn = 1024, 256
    return pl.pallas_call(
        _gate_kernel,
        grid=(N_TOK // tm, D_MODEL // tn, N_BRANCH),
        in_specs=[pl.BlockSpec((tm, D_MODEL), lambda i, j, n: (i, 0)),
                  pl.BlockSpec((tm, BRANCH_W), lambda i, j, n: (i, n)),
                  pl.BlockSpec((D_MODEL, tn), lambda i, j, n: (0, (LP_W + n * D_MODEL) // tn + j)),
                  pl.BlockSpec((None, None, BRANCH_W, tn), lambda i, j, n: (l, n, 0, j))],
        out_specs=pl.BlockSpec((tm, tn), lambda i, j, n: (i, j)),
        out_shape=jax.ShapeDtypeStruct((N_TOK, D_MODEL), BF16),
        scratch_shapes=[pltpu.VMEM((tm, tn), F32)],
        compiler_params=_params("arbitrary", "arbitrary", "arbitrary"),
        name="gated_branches",
    )(xn, u, w_rest, w_branch)


ROUTER_TM = 512


def _router_kernel(x_ref, w_ref, b_ref, tri_ref, idx_ref, gate_ref, rank_ref, cnt_ref, carry_ref):
    @pl.when(pl.program_id(0) == 0)
    def _():
        carry_ref[...] = jnp.zeros_like(carry_ref)

    logits = jnp.dot(x_ref[...].astype(BF16), w_ref[...], preferred_element_type=F32) + b_ref[...]
    lane = lax.broadcasted_iota(jnp.int32, logits.shape, 1).astype(F32)
    cur = jnp.where(lane < N_EXPERTS, logits, -jnp.inf)
    vals, idxs, hots = [], [], []
    for _ in range(TOP_K):
        m = jnp.max(cur, axis=-1, keepdims=True)
        idx = jnp.min(jnp.where(cur == m, lane, float(LANES)), axis=-1, keepdims=True)
        hot = lane == idx
        cur = jnp.where(hot, -jnp.inf, cur)
        vals.append(m)
        idxs.append(idx)
        hots.append(hot)
    exps = [jnp.exp(v - vals[0]) for v in vals]
    denom = exps[0] + exps[1] + exps[2] + exps[3]
    sel = jnp.zeros(logits.shape, F32)
    for hot in hots:
        sel = sel + hot.astype(F32)
    before = jnp.dot(tri_ref[...], sel.astype(BF16), preferred_element_type=F32) + carry_ref[...]
    idx_out = jnp.zeros(logits.shape, F32)
    gate_out = jnp.zeros(logits.shape, F32)
    rank_out = jnp.zeros(logits.shape, F32)
    for k in range(TOP_K):
        rank = jnp.sum(jnp.where(hots[k], before, 0.0), axis=-1, keepdims=True)
        idx_out = jnp.where(lane == k, idxs[k], idx_out)
        gate_out = jnp.where(lane == k, exps[k] / denom, gate_out)
        rank_out = jnp.where(lane == k, rank, rank_out)
    idx_ref[...] = idx_out.astype(jnp.int32)
    gate_ref[...] = gate_out
    rank_ref[...] = rank_out.astype(jnp.int32)
    carry_ref[...] += jnp.sum(sel, axis=0, keepdims=True)
    cnt_ref[...] = carry_ref[...].astype(jnp.int32)


def moe_route(xn2, router_w, router_b, l):
    tm = ROUTER_TM
    w = jnp.zeros((D_MODEL, LANES), BF16).at[:, :N_EXPERTS].set(router_w[l].astype(BF16))
    b = jnp.zeros((1, LANES), F32).at[0, :N_EXPERTS].set(router_b[l])
    tri = (lax.broadcasted_iota(jnp.int32, (tm, tm), 1)
           < lax.broadcasted_iota(jnp.int32, (tm, tm), 0)).astype(BF16)
    row = pl.BlockSpec((tm, LANES), lambda i: (i, 0))
    idx, gates, rank, cnt = pl.pallas_call(
        _router_kernel,
        grid=(N_TOK // tm,),
        in_specs=[pl.BlockSpec((tm, D_MODEL), lambda i: (i, 0)),
                  pl.BlockSpec((D_MODEL, LANES), lambda i: (0, 0)),
                  pl.BlockSpec((1, LANES), lambda i: (0, 0)),
                  pl.BlockSpec((tm, tm), lambda i: (0, 0))],
        out_specs=[row, row, row, pl.BlockSpec((1, LANES), lambda i: (0, 0))],
        out_shape=[jax.ShapeDtypeStruct((N_TOK, LANES), jnp.int32),
                   jax.ShapeDtypeStruct((N_TOK, LANES), F32),
                   jax.ShapeDtypeStruct((N_TOK, LANES), jnp.int32),
                   jax.ShapeDtypeStruct((1, LANES), jnp.int32)],
        scratch_shapes=[pltpu.VMEM((1, LANES), F32)],
        compiler_params=_params("arbitrary"),
        name="moe_route",
    )(xn2, w, b, tri)
    return idx[:, :TOP_K], gates[:, :TOP_K], rank[:, :TOP_K], cnt[0, :N_EXPERTS]


def _row_copy(src_hbm, dst_vmem, src_row, dst_row, sem):
    return pltpu.make_async_copy(src_hbm.at[pl.ds(src_row, 1), :], dst_vmem.at[pl.ds(dst_row, 1), :], sem)


def _gather_rows_kernel(tok_ref, nused_ref, x_hbm, o_ref, buf, sem):
    i = pl.program_id(0)
    base = i * MOE_ROWS_BLK

    @pl.when(i < nused_ref[0])
    def _():
        def start(r, c):
            _row_copy(x_hbm, buf, tok_ref[base + r], r, sem).start()
            return c

        def wait(r, c):
            _row_copy(x_hbm, buf, 0, r, sem).wait()
            return c

        lax.fori_loop(0, MOE_ROWS_BLK, start, 0, unroll=8)
        lax.fori_loop(0, MOE_ROWS_BLK, wait, 0, unroll=8)
        o_ref[...] = buf[...].astype(o_ref.dtype)

    @pl.when(i >= nused_ref[0])
    def _():
        o_ref[...] = jnp.zeros_like(o_ref)


def gather_rows(row_token, n_used, x):
    width = x.shape[1]
    return pl.pallas_call(
        _gather_rows_kernel,
        grid_spec=pltpu.PrefetchScalarGridSpec(
            num_scalar_prefetch=2,
            grid=(MOE_BLOCKS,),
            in_specs=[pl.BlockSpec(memory_space=pl.ANY)],
            out_specs=pl.BlockSpec((MOE_ROWS_BLK, width), lambda i, tok, nu: (i, 0)),
            scratch_shapes=[pltpu.VMEM((MOE_ROWS_BLK, width), x.dtype), pltpu.SemaphoreType.DMA(())]),
        out_shape=jax.ShapeDtypeStruct((MOE_ROWS, width), BF16),
        compiler_params=_params("arbitrary"),
        name="moe_gather_rows",
    )(row_token, n_used, x)


UP_TF = 512
DOWN_TN = 1024


def _expert_up_kernel(ie_ref, if_ref, ii_ref, first_ref, valid_ref,
                      x_ref, wg_ref, wu_ref, bg_ref, bu_ref, h_ref, wg_bf, wu_bf):
    q = pl.program_id(0)

    @pl.when(first_ref[q] == 1)
    def _():
        wg_bf[...] = wg_ref[...].astype(BF16)
        wu_bf[...] = wu_ref[...].astype(BF16)

    @pl.when(valid_ref[q] == 1)
    def _():
        x = x_ref[...]
        gate = jnp.dot(x, wg_bf[...], preferred_element_type=F32) + bg_ref[...]
        up = jnp.dot(x, wu_bf[...], preferred_element_type=F32) + bu_ref[...]
        gate = jnp.minimum(gate, SWIGLU_LIMIT)
        up = jnp.clip(up, -SWIGLU_LIMIT, SWIGLU_LIMIT)
        glu = gate * jax.nn.sigmoid(gate * SWIGLU_ALPHA)
        h_ref[...] = ((up + 1.0) * glu).astype(h_ref.dtype)

    @pl.when(valid_ref[q] == 0)
    def _():
        h_ref[...] = jnp.zeros_like(h_ref)


def expert_up(items, xs, w_gate, b_gate, w_up, b_up, l):
    tf = UP_TF
    n_items = items[0].shape[0]
    w_spec = pl.BlockSpec((None, None, D_MODEL, tf), lambda q, ie, jf, ii, fi, va: (l, ie[q], 0, jf[q]))
    b_spec = pl.BlockSpec((None, None, 1, tf), lambda q, ie, jf, ii, fi, va: (l, ie[q], 0, jf[q]))
    return pl.pallas_call(
        _expert_up_kernel,
        grid_spec=pltpu.PrefetchScalarGridSpec(
            num_scalar_prefetch=5,
            grid=(n_items,),
            in_specs=[pl.BlockSpec((MOE_ROWS_BLK, D_MODEL), lambda q, ie, jf, ii, fi, va: (ii[q], 0)),
                      w_spec, w_spec, b_spec, b_spec],
            out_specs=pl.BlockSpec((MOE_ROWS_BLK, tf), lambda q, ie, jf, ii, fi, va: (ii[q], jf[q])),
            scratch_shapes=[pltpu.VMEM((D_MODEL, tf), BF16), pltpu.VMEM((D_MODEL, tf), BF16)]),
        out_shape=jax.ShapeDtypeStruct((MOE_ROWS, EXPERT_FF), BF16),
        compiler_params=_params("arbitrary"),
        name="moe_expert_up",
    )(*items, xs, w_gate, w_up,
      b_gate.reshape(DEPTH, N_EXPERTS, 1, EXPERT_FF), b_up.reshape(DEPTH, N_EXPERTS, 1, EXPERT_FF))


def _expert_down_kernel(ie_ref, if_ref, ii_ref, first_ref, valid_ref, h_ref, wd_ref, bd_ref, y_ref, wd_bf):
    q = pl.program_id(0)

    @pl.when(first_ref[q] == 1)
    def _():
        wd_bf[...] = wd_ref[...].astype(BF16)

    @pl.when(valid_ref[q] == 1)
    def _():
        y_ref[...] = jnp.dot(h_ref[...], wd_bf[...], preferred_element_type=F32) + bd_ref[...]

    @pl.when(valid_ref[q] == 0)
    def _():
        y_ref[...] = jnp.zeros_like(y_ref)


def expert_down(items, h, w_down, b_down, l):
    tn = DOWN_TN
    n_items = items[0].shape[0]
    return pl.pallas_call(
        _expert_down_kernel,
        grid_spec=pltpu.PrefetchScalarGridSpec(
            num_scalar_prefetch=5,
            grid=(n_items,),
            in_specs=[pl.BlockSpec((MOE_ROWS_BLK, EXPERT_FF), lambda q, ie, jf, ii, fi, va: (ii[q], 0)),
                      pl.BlockSpec((None, None, EXPERT_FF, tn),
                                   lambda q, ie, jf, ii, fi, va: (l, ie[q], 0, jf[q])),
                      pl.BlockSpec((None, None, 1, tn), lambda q, ie, jf, ii, fi, va: (l, ie[q], 0, jf[q]))],
            out_specs=pl.BlockSpec((MOE_ROWS_BLK, tn), lambda q, ie, jf, ii, fi, va: (ii[q], jf[q])),
            scratch_shapes=[pltpu.VMEM((EXPERT_FF, tn), BF16)]),
        out_shape=jax.ShapeDtypeStruct((MOE_ROWS, D_MODEL), F32),
        compiler_params=_params("arbitrary"),
        name="moe_expert_down",
    )(*items, h, w_down, b_down.reshape(DEPTH, N_EXPERTS, 1, D_MODEL))


COMBINE_TM = 128


def _combine_kernel(pos_ref, y_hbm, g_ref, x_ref, gy_ref, gate_ref, g2_ref, sc_ref, sh_ref,
                    x1_ref, xn_ref, buf, sem):
    base = pl.program_id(0) * (COMBINE_TM * TOP_K)

    def start(r, c):
        for k in range(TOP_K):
            _row_copy(y_hbm, buf.at[k], pos_ref[base + r * TOP_K + k], r, sem).start()
        return c

    def wait(r, c):
        for k in range(TOP_K):
            _row_copy(y_hbm, buf.at[k], 0, r, sem).wait()
        return c

    lax.fori_loop(0, COMBINE_TM, start, 0)
    lax.fori_loop(0, COMBINE_TM, wait, 0)
    g = g_ref[...]
    ff = buf[0] * g[:, 0:1]
    for k in range(1, TOP_K):
        ff = ff + buf[k] * g[:, k:k + 1]
    x1 = x_ref[...] + gate_ref[...] * (_rms(ff) * gy_ref[...])
    x1_ref[...] = x1
    xn = _rms(x1) * g2_ref[...]
    xn_ref[...] = (xn * (1.0 + sc_ref[...]) + sh_ref[...]).astype(xn_ref.dtype)


def moe_combine(pos_flat, yb, gates, x, norms, mod, l, l2):
    tm = COMBINE_TM
    row = pl.BlockSpec((tm, D_MODEL), lambda i, p: (i, 0))

    def mod_spec(ll, which):
        return pl.BlockSpec((None, None, None, 1, D_MODEL),
                            lambda i, p: (ll, _group_of_tile(i, tm), which, 0, 0))

    def norm_spec(ll, which):
        return pl.BlockSpec((None, None, 1, D_MODEL), lambda i, p: (ll, which, 0, 0))

    return pl.pallas_call(
        _combine_kernel,
        grid_spec=pltpu.PrefetchScalarGridSpec(
            num_scalar_prefetch=1,
            grid=(N_TOK // tm,),
            in_specs=[pl.BlockSpec(memory_space=pl.ANY),
                      pl.BlockSpec((tm, TOP_K), lambda i, p: (i, 0)),
                      row, norm_spec(l, 3), mod_spec(l, 5), norm_spec(l2, 0), mod_spec(l2, 1), mod_spec(l2, 0)],
            out_specs=[row, row],
            scratch_shapes=[pltpu.VMEM((TOP_K, tm, D_MODEL), F32), pltpu.SemaphoreType.DMA(())]),
        out_shape=[jax.ShapeDtypeStruct((N_TOK, D_MODEL), F32),
                   jax.ShapeDtypeStruct((N_TOK, D_MODEL), BF16)],
        compiler_params=_params("arbitrary"),
        name="moe_combine",
    )(pos_flat, yb, gates, x, norms, mod, norms, mod, mod)


def _work_items(n_blk_e, n_tiles):
    total = MOE_BLOCKS * n_tiles
    blk_end = jnp.cumsum(n_blk_e)
    blk_start = blk_end - n_blk_e
    n_used = blk_end[-1]
    used = n_used * n_tiles
    q = jnp.arange(total, dtype=jnp.int32)
    valid = q < used
    qc = jnp.minimum(q, used - 1)
    e = jnp.searchsorted(blk_end * n_tiles, qc, side="right").astype(jnp.int32)
    rel = qc - blk_start[e] * n_tiles
    nb = jnp.maximum(n_blk_e[e], 1)
    n_tail = jnp.maximum(MOE_BLOCKS - n_used, 1)
    rel_tail = q - used
    tile = jnp.where(valid, rel // nb, rel_tail // n_tail)
    blk = jnp.where(valid, blk_start[e] + rel % nb, n_used + rel_tail % n_tail)
    first = (rel % nb == 0) & valid
    return (e, tile.astype(jnp.int32), blk.astype(jnp.int32), first.astype(jnp.int32),
            valid.astype(jnp.int32))


def moe_ffn(xn2, x1, norms, mod, l, l2, router_w, router_b, w_gate, b_gate, w_up, b_up, w_down, b_down):
    idx, gates, rank, counts = moe_route(xn2, router_w, router_b, l)
    n_blk_e = (counts + MOE_ROWS_BLK - 1) // MOE_ROWS_BLK
    pstart = (jnp.cumsum(n_blk_e) - n_blk_e) * MOE_ROWS_BLK
    pos = pstart[idx] + rank
    tok = jnp.broadcast_to(jnp.arange(N_TOK, dtype=jnp.int32)[:, None], pos.shape)
    row_token = jnp.zeros((MOE_ROWS,), jnp.int32).at[pos.reshape(-1)].set(tok.reshape(-1))
    xs = gather_rows(row_token, jnp.sum(n_blk_e).reshape(1).astype(jnp.int32), xn2)
    h = expert_up(_work_items(n_blk_e, EXPERT_FF // UP_TF), xs, w_gate, b_gate, w_up, b_up, l)
    yb = expert_down(_work_items(n_blk_e, D_MODEL // DOWN_TN), h, w_down, b_down, l)
    return moe_combine(pos.reshape(-1).astype(jnp.int32), yb, gates, x1, norms, mod, l, l2)


HALF = RWKV_HEAD
WKV_L = 64
SOLVE_BLK = 16
N_PAIR = BRANCH_W // LANES
MIX_TM = 256


def _head_sum(x):
    li = lax.broadcasted_iota(jnp.int32, (LANES, LANES), 0) // HALF
    lj = lax.broadcasted_iota(jnp.int32, (LANES, LANES), 1) // HALF
    ones_bd = (li == lj).astype(BF16)
    hi, lo = _split_bf16(x)
    parts = []
    for p in range(x.shape[1] // LANES):
        sl = slice(p * LANES, (p + 1) * LANES)
        parts.append(jnp.dot(hi[:, sl], ones_bd, preferred_element_type=F32)
                     + jnp.dot(lo[:, sl], ones_bd, preferred_element_type=F32))
    return jnp.concatenate(parts, axis=1)


def _rwkv_prep_kernel(z_ref, zp_ref, zn_ref, vf_ref, mu_ref, w0_ref, w2_ref, a0_ref, a2_ref, kk_ref, ka_ref,
                      rk_ref, g2_ref, v0_ref, v1_ref, v2_ref,
                      r_o, v_o, kk_o, g_o, bonus_o, lwf_o, kf_o, bf_o, lwb_o, kb_o, bb_o,
                      *, n_ctx_blocks, lat_blocks, has_vres):
    i = pl.program_id(0)
    j = jnp.maximum(i - n_ctx_blocks, 0) % lat_blocks
    is_lat = i >= n_ctx_blocks
    has_prev = jnp.logical_and(is_lat, j > 0).astype(F32)
    has_next = jnp.logical_and(is_lat, j < lat_blocks - 1).astype(F32)
    z = z_ref[...]
    tm = z.shape[0]
    row = lax.broadcasted_iota(jnp.int32, z.shape, 0)
    prev_row = zp_ref[7:8, :] * has_prev
    next_row = zn_ref[0:1, :] * has_next
    z_up = jnp.where(row == 0, prev_row, pltpu.roll(z, 1, 0))
    z_dn = jnp.where(row == tm - 1, next_row, pltpu.roll(z, tm - 1, 0))
    z = z + mu_ref[...] * (0.5 * (z_up + z_dn) - z)

    C = BRANCH_W
    r, k, v = z[:, 0:C], z[:, C:2 * C], z[:, 2 * C:3 * C]
    lora = z[:, 3 * C:3 * C + 2 * LANES]
    gd = z[:, 3 * C + 2 * LANES:3 * C + 4 * LANES]
    if has_vres:
        mixv = jax.nn.sigmoid(v0_ref[...] + _dot(_dot(v, v1_ref[...]), v2_ref[...]))
        v = v + (vf_ref[...] - v) * mixv
    g = _dot(jax.nn.sigmoid(gd), g2_ref[...])
    kk = k * kk_ref[...]
    kk = kk / jnp.maximum(jnp.sqrt(_head_sum(kk * kk)), 1e-12)
    r_o[...] = r
    v_o[...] = v
    kk_o[...] = kk
    g_o[...] = g
    bonus_o[...] = _head_sum(r * k * rk_ref[...]) * v
    tw = jnp.tanh(lora[:, :LANES])
    ad = lora[:, LANES:]
    for d, (lw_o, k_o, b_o) in enumerate(((lwf_o, kf_o, bf_o), (lwb_o, kb_o, bb_o))):
        w_log = -_softplus(-(w0_ref[d:d + 1, :] + _dot(tw, w2_ref[d]))) - 0.5
        lw_o[...] = -jnp.exp(w_log)
        rate = jax.nn.sigmoid(a0_ref[d:d + 1, :] + _dot(ad, a2_ref[d]))
        k_o[...] = k * (1.0 + (rate - 1.0) * ka_ref[d:d + 1, :])
        b_o[...] = kk * rate


def rwkv_prep(z_r, v_first, l, n_ctx_blocks, lat_blocks, rwkv_mu, rwkv_w0, rwkv_w2, rwkv_a0, rwkv_a2,
              rwkv_kk, rwkv_ka, rwkv_rk, rwkv_g2, rwkv_v0, rwkv_v1, rwkv_v2):
    n = z_r.shape[0]
    tm = MIX_TM
    has_vres = l > 0
    C = BRANCH_W
    mu = jnp.zeros((1, RWKV_PAD), F32).at[0, :RWKV_IN].set(rwkv_mu[l])
    w2 = jnp.zeros((2, LANES, C), F32)
    a2 = jnp.zeros((2, LANES, C), F32)
    for d in range(2):
        w2 = w2.at[d, d * RWKV_W_RANK:(d + 1) * RWKV_W_RANK].set(rwkv_w2[l, d])
        a2 = a2.at[d, d * RWKV_A_RANK:(d + 1) * RWKV_A_RANK].set(rwkv_a2[l, d])
    g2 = jnp.zeros((2 * LANES, C), F32).at[:RWKV_G_RANK].set(rwkv_g2[l])
    lv = max(l - 1, 0)
    v0 = rwkv_v0[lv].reshape(1, C)
    v1 = jnp.zeros((C, LANES), F32).at[:, :RWKV_V_RANK].set(rwkv_v1[lv])
    v2 = jnp.zeros((LANES, C), F32).at[:RWKV_V_RANK].set(rwkv_v2[lv])
    if v_first is None:
        v_first = jnp.zeros((8, C), F32)
        vf_spec = pl.BlockSpec((8, C), lambda i: (0, 0))
    else:
        vf_spec = pl.BlockSpec((tm, C), lambda i: (i, 0))
    halo = tm // 8
    last8 = n // 8 - 1

    def full(shape):
        return pl.BlockSpec(shape, lambda i: (0,) * len(shape))

    out_row = pl.BlockSpec((tm, C), lambda i: (i, 0))
    return pl.pallas_call(
        functools.partial(_rwkv_prep_kernel, n_ctx_blocks=n_ctx_blocks, lat_blocks=lat_blocks,
                          has_vres=has_vres),
        grid=(n // tm,),
        in_specs=[pl.BlockSpec((tm, RWKV_PAD), lambda i: (i, 0)),
                  pl.BlockSpec((8, RWKV_PAD), lambda i: (jnp.maximum(i * halo - 1, 0), 0)),
                  pl.BlockSpec((8, RWKV_PAD), lambda i: (jnp.minimum((i + 1) * halo, last8), 0)),
                  vf_spec, full((1, RWKV_PAD)), full((2, C)), full((2, LANES, C)), full((2, C)),
                  full((2, LANES, C)), full((1, C)), full((2, C)), full((1, C)), full((2 * LANES, C)),
                  full((1, C)), full((C, LANES)), full((LANES, C))],
        out_specs=[out_row] * 11,
        out_shape=[jax.ShapeDtypeStruct((n, C), F32)] * 11,
        compiler_params=_params("arbitrary"),
        name="rwkv_prep",
    )(z_r, z_r, z_r, v_first, mu, rwkv_w0[l], w2, rwkv_a0[l], a2, rwkv_kk[l].reshape(1, C), rwkv_ka[l],
      rwkv_rk[l].reshape(1, C), g2, v0, v1, v2)


def _bd(x):
    lane = lax.broadcasted_iota(jnp.int32, x.shape, 1)
    return jnp.concatenate([jnp.where(lane < HALF, x, 0.0), jnp.where(lane >= HALF, x, 0.0)], axis=0)


def _dot_hp(a, b):
    a_hi, a_lo = _split_bf16(a)
    b_hi, b_lo = _split_bf16(b)
    bd_hi = _bd(b_hi)
    return (jnp.dot(a_hi, bd_hi, preferred_element_type=F32)
            + jnp.dot(a_lo, bd_hi, preferred_element_type=F32)
            + jnp.dot(a_hi, _bd(b_lo), preferred_element_type=F32))


def _wkv_direction(r, v, kk, lw, k, b, s_ref, d, rev, y_ref):
    L = WKV_L
    ti = lax.broadcasted_iota(jnp.int32, (L, L), 0)
    si = lax.broadcasted_iota(jnp.int32, (L, L), 1)
    tri = ((si >= ti) if rev else (si <= ti)).astype(BF16)
    cs = _dot_left_exact(tri, lw)
    ctot = cs[0:1] if rev else cs[L - 1:L]
    e_neg = jnp.exp(-cs)
    rt = r * jnp.exp(cs)
    at = -kk * jnp.exp(cs - lw)
    bt = b * e_neg
    kt = k * e_neg
    e_rem = jnp.exp(ctot - cs)
    bh = b * e_rem
    kh = k * e_rem
    wtot = jnp.exp(ctot)
    row = lax.broadcasted_iota(jnp.int32, (L, LANES), 0)
    sidx = lax.broadcasted_iota(jnp.int32, (L, LANES), 1) % HALF
    strict = (sidx > row) if rev else (sidx < row)
    incl = (sidx >= row) if rev else (sidx <= row)
    eye = (sidx == row).astype(F32)
    same_head = (lax.broadcasted_iota(jnp.int32, (LANES, LANES), 0) // HALF
                 == lax.broadcasted_iota(jnp.int32, (LANES, LANES), 1) // HALF)
    pairs = range(N_PAIR)
    sls = [slice(p * LANES, (p + 1) * LANES) for p in pairs]
    S = [s_ref[d, p] for p in pairs]
    ar = [jnp.concatenate([at[:, sl], rt[:, sl]], axis=0) for sl in sls]
    g_b = [_dot_nt(ar[p], _bd(bt[:, sls[p]])) for p in pairs]
    g_k = [_dot_nt(ar[p], _bd(kt[:, sls[p]])) for p in pairs]
    g_s = [_dot_nt(ar[p], S[p]) for p in pairs]
    n_ab = [jnp.where(strict, g_b[p][:L], 0.0) for p in pairs]
    same_blk = (sidx // SOLVE_BLK) == (row // SOLVE_BLK)
    pw = [jnp.where(same_blk, n_ab[p], 0.0) for p in pairs]
    dinv = [eye + pw[p] for p in pairs]
    step = 2
    while step < SOLVE_BLK:
        pw = [_dot_hp(pw[p], pw[p]) for p in pairs]
        dinv = [dinv[p] + _dot_hp(dinv[p], pw[p]) for p in pairs]
        step *= 2
    vb = [_bd(v[:, sl]) for sl in sls]
    l_ak = [jnp.where(strict, g_k[p][:L], 0.0) for p in pairs]
    rhs = [g_s[p][:L] + _dot(l_ak[p], vb[p]) for p in pairs]
    n_blk = L // SOLVE_BLK
    zeros_blk = jnp.zeros((SOLVE_BLK, LANES), F32)
    done = [[None] * n_blk for _ in pairs]

    def rows_with(blocks):
        return jnp.concatenate([zeros_blk if b is None else b for b in blocks], axis=0)

    for kb in (range(n_blk - 1, -1, -1) if rev else range(n_blk)):
        rows = slice(kb * SOLVE_BLK, (kb + 1) * SOLVE_BLK)
        for p in pairs:
            acc = rhs[p][rows]
            if any(b is not None for b in done[p]):
                acc = acc + _dot(n_ab[p][rows], _bd(rows_with(done[p])))
            only = [None] * n_blk
            only[kb] = acc
            done[p][kb] = _dot(dinv[p][rows], _bd(rows_with(only)))
    ut = [rows_with(done[p]) for p in pairs]
    for p in pairs:
        l_rb = jnp.where(incl, g_b[p][L:], 0.0)
        l_rk = jnp.where(incl, g_k[p][L:], 0.0)
        y_ref[:, sls[p]] = g_s[p][L:] + _dot(l_rb, _bd(ut[p])) + _dot(l_rk, vb[p])
    for p in pairs:
        upd = _dot(jnp.concatenate([ut[p], v[:, sls[p]]], axis=0).T,
                   jnp.concatenate([bh[:, sls[p]], kh[:, sls[p]]], axis=0))
        s_ref[d, p] = S[p] * wtot[:, sls[p]] + jnp.where(same_head, upd, 0.0)


def _wkv_kernel(rf_ref, vf_ref, kkf_ref, lwf_ref, kf_ref, bf_ref, rb_ref, vb_ref, kkb_ref, lwb_ref, kb_ref,
                bb_ref, s0_ref, yf_ref, yb_ref, st_ref, s_scr):
    c = pl.program_id(1)

    @pl.when(c == 0)
    def _():
        s_scr[...] = s0_ref[...]

    _wkv_direction(rf_ref[...], vf_ref[...], kkf_ref[...], lwf_ref[...], kf_ref[...], bf_ref[...],
                   s_scr, 0, False, yf_ref)
    _wkv_direction(rb_ref[...], vb_ref[...], kkb_ref[...], lwb_ref[...], kb_ref[...], bb_ref[...],
                   s_scr, 1, True, yb_ref)

    @pl.when(c == pl.num_programs(1) - 1)
    def _():
        st_ref[...] = s_scr[...]


def wkv_scan(prep, s0_bd, row_off, n_seq, t_len):
    r, v, kk, lwf, kf, bf, lwb, kb, bb = prep
    L = WKV_L
    nc = t_len // L
    base = row_off // L
    rows = n_seq * t_len
    fwd = pl.BlockSpec((L, BRANCH_W), lambda s, c: (base + s * nc + c, 0))
    bwd = pl.BlockSpec((L, BRANCH_W), lambda s, c: (base + s * nc + (nc - 1 - c), 0))
    y_fwd = pl.BlockSpec((L, BRANCH_W), lambda s, c: (s * nc + c, 0))
    y_bwd = pl.BlockSpec((L, BRANCH_W), lambda s, c: (s * nc + (nc - 1 - c), 0))
    st = pl.BlockSpec((None, 2, N_PAIR, LANES, LANES), lambda s, c: (s, 0, 0, 0, 0))
    return pl.pallas_call(
        _wkv_kernel,
        grid=(n_seq, nc),
        in_specs=[fwd] * 6 + [bwd] * 6 + [st],
        out_specs=[y_fwd, y_bwd, st],
        out_shape=[jax.ShapeDtypeStruct((rows, BRANCH_W), F32), jax.ShapeDtypeStruct((rows, BRANCH_W), F32),
                   jax.ShapeDtypeStruct((n_seq, 2, N_PAIR, LANES, LANES), F32)],
        scratch_shapes=[pltpu.VMEM((2, N_PAIR, LANES, LANES), F32)],
        compiler_params=_params("arbitrary", "arbitrary"),
        name="wkv_scan",
    )(r, v, kk, lwf, kf, bf, r, v, kk, lwb, kb, bb, s0_bd)


def pair_states(S):
    out = jnp.zeros((S.shape[0], 2, N_PAIR, LANES, LANES), F32)
    out = out.at[:, :, :, :HALF, :HALF].set(S[:, :, 0::2])
    return out.at[:, :, :, HALF:, HALF:].set(S[:, :, 1::2])


def unpair_states(Sp):
    both = jnp.stack([Sp[:, :, :, :HALF, :HALF], Sp[:, :, :, HALF:, HALF:]], axis=3)
    return both.reshape(Sp.shape[0], 2, RWKV_H, RWKV_HEAD, RWKV_HEAD)


def _rwkv_post_kernel(yf_ref, yb_ref, bonus_ref, g_ref, gw_ref, gb_ref, o_ref):
    y = yf_ref[...] + yb_ref[...]
    inv_n = 1.0 / RWKV_HEAD
    d = y - _head_sum(y) * inv_n
    var = _head_sum(d * d) * inv_n
    yn = d * lax.rsqrt(var + RWKV_GN_EPS) * gw_ref[...] + gb_ref[...]
    o_ref[...] = ((yn + bonus_ref[...]) * g_ref[...]).astype(o_ref.dtype)


def rwkv_post(yf, yb, bonus, g, gn_w, gn_b):
    n = yf.shape[0]
    tm = MIX_TM
    C = BRANCH_W
    row = pl.BlockSpec((tm, C), lambda i: (i, 0))
    vec = pl.BlockSpec((1, C), lambda i: (0, 0))
    return pl.pallas_call(
        _rwkv_post_kernel,
        grid=(n // tm,),
        in_specs=[row, row, row, row, vec, vec],
        out_specs=row,
        out_shape=jax.ShapeDtypeStruct((n, C), BF16),
        compiler_params=_params("arbitrary"),
        name="rwkv_post",
    )(yf, yb, bonus, g, gn_w.reshape(1, C), gn_b.reshape(1, C))


def _expm1(x):
    small = x * (1.0 + x * (0.5 + x * (1.0 / 6.0 + x * (1.0 / 24.0 + x * (1.0 / 120.0)))))
    return jnp.where(jnp.abs(x) < 0.1, small, jnp.exp(x) - 1.0)


def _gelu_tanh(x):
    return 0.5 * x * (1.0 + jnp.tanh(0.7978845608028654 * (x + 0.044715 * x * x * x)))


def _lru_kernel(zx_ref, zg_ref, h0_ref, cw_ref, cb_ref, wa_ref, ba_ref, wx_ref, bx_ref, lam_ref,
                y_ref, hT_ref, af, hf, ab, hb):
    T = zx_ref.shape[0]
    x = zx_ref[...]
    row = lax.broadcasted_iota(jnp.int32, x.shape, 0)
    xc = cb_ref[...] + x * cw_ref[1:2, :]
    xc = xc + jnp.where(row >= 1, pltpu.roll(x, 1, 0), 0.0) * cw_ref[0:1, :]
    xc = xc + jnp.where(row < T - 1, pltpu.roll(x, T - 1, 0), 0.0) * cw_ref[2:3, :]
    xc = xc + jnp.where(row < T - 2, pltpu.roll(x, T - 2, 0), 0.0) * cw_ref[3:4, :]
    for d, (a_s, h_s) in enumerate(((af, hf), (ab, hb))):
        rg = jax.nn.sigmoid(_dot(xc, wa_ref[d]) + ba_ref[d:d + 1, :])
        ig = jax.nn.sigmoid(_dot(xc, wx_ref[d]) + bx_ref[d:d + 1, :])
        log_a = -LRU_C * rg * _softplus(-lam_ref[d:d + 1, :])
        a_s[...] = jnp.exp(log_a)
        h_s[...] = jnp.sqrt(-_expm1(2.0 * log_a)) * (ig * xc)

    def step(t, carry):
        h_f, h_b = carry
        tb = T - 1 - t
        h_f = af[pl.ds(t, 1), :] * h_f + hf[pl.ds(t, 1), :]
        h_b = ab[pl.ds(tb, 1), :] * h_b + hb[pl.ds(tb, 1), :]
        hf[pl.ds(t, 1), :] = h_f
        hb[pl.ds(tb, 1), :] = h_b
        return h_f, h_b

    h_f, h_b = lax.fori_loop(0, T, step, (h0_ref[0:1, :], h0_ref[1:2, :]), unroll=8)
    hT_ref[0:1, :] = h_f
    hT_ref[1:2, :] = h_b
    y_ref[...] = ((hf[...] + hb[...]) * _gelu_tanh(zg_ref[...])).astype(y_ref.dtype)


def rglru_mixer(z_lp, h0, row_off, n_seq, t_len, l, lru_conv_w, lru_conv_b, lru_wa, lru_ba, lru_wx, lru_bx,
                lru_lam):
    W, BW = LRU_W, LRU_BW
    base = row_off // t_len
    st = pl.BlockSpec((None, 2, BW), lambda s, b: (s, 0, b))
    vec2 = pl.BlockSpec((None, 2, BW), lambda s, b: (l, 0, b))
    mat = pl.BlockSpec((None, 2, None, BW, BW), lambda s, b: (l, 0, b, 0, 0))
    return pl.pallas_call(
        _lru_kernel,
        grid=(n_seq, LRU_BLOCKS),
        in_specs=[pl.BlockSpec((t_len, BW), lambda s, b: (base + s, b)),
                  pl.BlockSpec((t_len, BW), lambda s, b: (base + s, LRU_BLOCKS + b)),
                  st,
                  pl.BlockSpec((None, LRU_CONV, BW), lambda s, b: (l, 0, b)),
                  pl.BlockSpec((None, 1, BW), lambda s, b: (l, 0, b)),
                  mat, vec2, mat, vec2, vec2],
        out_specs=[pl.BlockSpec((t_len, BW), lambda s, b: (s, b)), st],
        out_shape=[jax.ShapeDtypeStruct((n_seq * t_len, W), BF16), jax.ShapeDtypeStruct((n_seq, 2, W), F32)],
        scratch_shapes=[pltpu.VMEM((t_len, BW), F32)] * 4,
        compiler_params=_params("arbitrary", "arbitrary"),
        name="rglru_mixer",
    )(z_lp, z_lp, h0, lru_conv_w, lru_conv_b.reshape(DEPTH, 1, W), lru_wa, lru_ba, lru_wx, lru_bx, lru_lam)


def _fnet_kernel(z_ref, ct_ref, st_ref, cc_ref, sc_ref, w_ref, o_ref):
    z = z_ref[...]
    T = z.shape[0]
    zc = _dot(z, cc_ref[...])
    zs = _dot(z, sc_ref[...])
    f = (_dot(ct_ref[...], zc) - _dot(st_ref[...], zs)) * (1.0 / (T * FNET_GW) ** 0.5)
    o_ref[...] = _dot(f, w_ref[...]).astype(o_ref.dtype)


def _dft(n):
    idx = jnp.arange(n, dtype=jnp.int32)
    ang = ((idx[:, None] * idx[None, :]) % n).astype(F32) * (2.0 * jnp.pi / n)
    return jnp.cos(ang).astype(BF16), jnp.sin(ang).astype(BF16)


def fourier_mixer(z_f, row_off, n_seq, t_len, l, fnet_w):
    base = row_off // t_len
    ct, st = _dft(t_len)
    cc, sc = _dft(FNET_GW)
    sq = lambda n: pl.BlockSpec((n, n), lambda s, g: (0, 0))
    return pl.pallas_call(
        _fnet_kernel,
        grid=(n_seq, FNET_GROUPS),
        in_specs=[pl.BlockSpec((t_len, FNET_GW), lambda s, g: (base + s, g)),
                  sq(t_len), sq(t_len), sq(FNET_GW), sq(FNET_GW),
                  pl.BlockSpec((None, None, FNET_GW, FNET_GW), lambda s, g: (l, g, 0, 0))],
        out_specs=pl.BlockSpec((t_len, FNET_GW), lambda s, g: (s, g)),
        out_shape=jax.ShapeDtypeStruct((n_seq * t_len, BRANCH_W), BF16),
        compiler_params=_params("arbitrary", "arbitrary"),
        name="fourier_mixer",
    )(z_f, ct, st, cc, sc, fnet_w)


def _pool_kernel(z_ref, w_ref, sc_ref, o_ref):
    z = z_ref[...]
    T = z.shape[0]
    half = jnp.left_shift(1, pl.program_id(1))
    ti = lax.broadcasted_iota(jnp.int32, (T, T), 0)
    si = lax.broadcasted_iota(jnp.int32, (T, T), 1)
    band = jnp.logical_and(si >= ti - half, si < ti + half).astype(BF16)
    t1 = lax.broadcasted_iota(jnp.int32, (T, 1), 0)
    cnt = (jnp.minimum(t1 + half, T) - jnp.maximum(t1 - half, 0)).astype(F32)
    d = _dot_left_exact(band, z) / cnt - z
    o_ref[...] = (_dot(d, w_ref[...]) * sc_ref[...]).astype(o_ref.dtype)


def pool_mixer(z_lp, row_off, n_seq, t_len, l, pool_w, pool_scale):
    base = row_off // t_len
    col0 = 2 * LRU_W // POOL_GW
    return pl.pallas_call(
        _pool_kernel,
        grid=(n_seq, len(POOL_WINDOWS)),
        in_specs=[pl.BlockSpec((t_len, POOL_GW), lambda s, g: (base + s, col0 + g)),
                  pl.BlockSpec((None, None, POOL_GW, POOL_GW), lambda s, g: (l, g, 0, 0)),
                  pl.BlockSpec((None, 1, POOL_GW), lambda s, g: (l, 0, g))],
        out_specs=pl.BlockSpec((t_len, POOL_GW), lambda s, g: (s, g)),
        out_shape=jax.ShapeDtypeStruct((n_seq * t_len, BRANCH_W), BF16),
        compiler_params=_params("arbitrary", "arbitrary"),
        name="pool_mixer",
    )(z_lp, pool_w, pool_scale.reshape(DEPTH, 1, BRANCH_W))


def _grid_position_code(n_tokens):
    rows = n_tokens // GRID_W
    rr, cc = jnp.meshgrid(jnp.arange(rows, dtype=F32), jnp.arange(GRID_W, dtype=F32), indexing="ij")
    quarter = D_MODEL // 4
    omega = POS_BASE ** (-jnp.arange(quarter, dtype=F32) / quarter)

    def enc(pos):
        ang = pos.reshape(-1)[:, None] * omega[None, :]
        return jnp.concatenate([jnp.sin(ang), jnp.cos(ang)], axis=-1)

    return jnp.concatenate([enc(rr), enc(cc)], axis=-1)


def kernel(x_prompt, x_sample, state_rwkv, state_lru, c, c_ctx, ada_w, ada_b, norms, w_in, fnet_w, rwkv_mu, rwkv_w0, rwkv_w2, rwkv_a0, rwkv_a2, rwkv_kk, rwkv_ka, rwkv_rk, rwkv_g2, rwkv_gn_w, rwkv_gn_b, rwkv_v0, rwkv_v1, rwkv_v2, lru_conv_w, lru_conv_b, lru_wa, lru_ba, lru_wx, lru_bx, lru_lam, pool_w, pool_scale, w_branch, w_out, router_w, router_b, moe_w_gate, moe_b_gate, moe_w_up, moe_b_up, moe_w_down, moe_b_down):
    x_lat = x_sample + _grid_position_code(DEC_SEQ).astype(x_sample.dtype)[None]
    x = jnp.concatenate([x_prompt.reshape(N_CTX, D_MODEL), x_lat.reshape(N_LAT, D_MODEL)], axis=0)

    cvec = jnp.zeros((8, D_MODEL), F32).at[0].set(c_ctx).at[1:1 + DEC_BATCH].set(c)
    mod = ada_modulation(cvec, ada_w, ada_b)[:, :N_GROUPS].reshape(DEPTH, N_GROUPS, 6, 1, D_MODEL)
    norms4 = norms.reshape(DEPTH, 4, 1, D_MODEL)

    groups = ((0, BATCH, SEQ), (N_CTX, DEC_BATCH, DEC_SEQ))
    s0_ctx = jnp.zeros((BATCH, 2, N_PAIR, LANES, LANES), F32)
    h0_ctx = jnp.zeros((BATCH, 2, LRU_W), F32)
    v_first = None
    S_ctx, h_ctx = [], []

    xn = norm_mod(x, norms4, mod, 0, 0, 1, 0)
    for l in range(DEPTH):
        z_f = matmul_cols(xn, w_in, (l,), 0, BRANCH_W, F32, "in_proj_fourier")
        z_r = matmul_cols(xn, w_in, (l,), BRANCH_W, RWKV_PAD, F32, "in_proj_rwkv")
        w_rest = w_in[l, :, REST_OFF:]
        z_lp = matmul_cols(xn, w_rest, (), 0, LP_W, F32, "in_proj_lru_pool")

        r, v, kk, g, bonus, lwf, kf, bf, lwb, kb, bb = rwkv_prep(
            z_r, v_first, l, N_CTX // MIX_TM, DEC_SEQ // MIX_TM, rwkv_mu, rwkv_w0, rwkv_w2, rwkv_a0, rwkv_a2,
            rwkv_kk, rwkv_ka, rwkv_rk, rwkv_g2, rwkv_v0, rwkv_v1, rwkv_v2)
        if l == 0:
            v_first = v
        scan_in = (r, v, kk, lwf, kf, bf, lwb, kb, bb)
        s0 = (s0_ctx, pair_states(state_rwkv[:, l]))
        h0 = (h0_ctx, state_lru[:, l])
        yf, yb, u_f, u_l, u_p = [], [], [], [], []
        for gi, (row0, n_seq, t_len) in enumerate(groups):
            yf_g, yb_g, s_g = wkv_scan(scan_in, s0[gi], row0, n_seq, t_len)
            yf.append(yf_g)
            yb.append(yb_g)
            u_f.append(fourier_mixer(z_f, row0, n_seq, t_len, l, fnet_w))
            u_l_g, h_g = rglru_mixer(z_lp, h0[gi], row0, n_seq, t_len, l, lru_conv_w, lru_conv_b, lru_wa,
                                     lru_ba, lru_wx, lru_bx, lru_lam)
            u_l.append(u_l_g)
            u_p.append(pool_mixer(z_lp, row0, n_seq, t_len, l, pool_w, pool_scale))
            if gi == 0:
                S_ctx.append(unpair_states(s_g))
                h_ctx.append(h_g)
        cat = lambda parts: jnp.concatenate(parts, axis=0)
        u_r = rwkv_post(cat(yf), cat(yb), bonus, g, rwkv_gn_w[l], rwkv_gn_b[l])
        u = jnp.concatenate([cat(u_f), u_r, cat(u_l), cat(u_p)], axis=1)
        gated = gated_branches(xn, u, w_rest, w_branch, l)
        mix = matmul_cols(gated, w_out, (l,), 0, D_MODEL, F32, "out_proj")
        x1, xn2 = resid_norm(x, mix, norms4, mod, l, 1, 2, l, 2, 4, 3)
        l2 = min(l + 1, DEPTH - 1)
        x, xn = moe_ffn(xn2, x1, norms4, mod, l, l2, router_w, router_b, moe_w_gate, moe_b_gate,
                        moe_w_up, moe_b_up, moe_w_down, moe_b_down)

    y_prompt = x[:N_CTX].reshape(BATCH, SEQ, D_MODEL)
    y_sample = x[N_CTX:].reshape(DEC_BATCH, DEC_SEQ, D_MODEL)
    return (y_prompt, y_sample, jnp.stack(S_ctx, axis=1), jnp.stack(h_ctx, axis=1))
```

```python
import functools

import jax
import jax.numpy as jnp
from jax import lax
from jax.experimental import pallas as pl
from jax.experimental.pallas import tpu as pltpu

D_MODEL = 4096
BATCH = 32
SEQ = 256
DEPTH = 2
DEC_BATCH = 2
DEC_SEQ = 1024

GRID_W = 64
N_BRANCH = 4
BRANCH_W = D_MODEL // 4
FNET_GROUPS = 4
FNET_GW = BRANCH_W // FNET_GROUPS
RWKV_HEAD = 64
RWKV_H = BRANCH_W // RWKV_HEAD
RWKV_W_RANK = 64
RWKV_A_RANK = 64
RWKV_V_RANK = 32
RWKV_G_RANK = 160
RWKV_GN_EPS = 64e-5
RWKV_IN = 3 * BRANCH_W + 2 * RWKV_W_RANK + 2 * RWKV_A_RANK + RWKV_G_RANK
LRU_W = BRANCH_W
LRU_BLOCKS = 4
LRU_BW = LRU_W // LRU_BLOCKS
LRU_CONV = 4
LRU_C = 8.0
POOL_WINDOWS = (2, 4, 8, 16)
POOL_GW = BRANCH_W // len(POOL_WINDOWS)
P_IN = BRANCH_W + RWKV_IN + 2 * LRU_W + BRANCH_W + N_BRANCH * D_MODEL
N_EXPERTS = 32
TOP_K = 4
EXPERT_FF = D_MODEL // 2
SWIGLU_LIMIT = 7.0
SWIGLU_ALPHA = 1.702
RMS_EPS = 1e-6
POS_BASE = 10000.0

LANES = 128
N_CTX = BATCH * SEQ
N_LAT = DEC_BATCH * DEC_SEQ
N_TOK = N_CTX + N_LAT
N_GROUPS = 1 + DEC_BATCH
RWKV_PAD = -(-RWKV_IN // LANES) * LANES
REST_OFF = BRANCH_W + RWKV_IN
LP_W = 2 * LRU_W + BRANCH_W
MOE_ROWS_BLK = 256
MOE_BLOCKS = -(-(N_TOK * TOP_K) // MOE_ROWS_BLK) + N_EXPERTS
MOE_ROWS = MOE_BLOCKS * MOE_ROWS_BLK
VMEM_LIMIT = 56 * 1024 * 1024

F32 = jnp.float32
BF16 = jnp.bfloat16


def _params(*sem):
    return pltpu.CompilerParams(dimension_semantics=sem, vmem_limit_bytes=VMEM_LIMIT)


def _group_of_tile(i, tm):
    n_ctx_tiles = N_CTX // tm
    return jnp.where(i < n_ctx_tiles, 0, 1 + (i - n_ctx_tiles) // (DEC_SEQ // tm))


def _rms(x):
    return x * lax.rsqrt(jnp.mean(x * x, axis=-1, keepdims=True) + RMS_EPS)


def _dot(a, b):
    return jnp.dot(a.astype(BF16), b.astype(BF16), preferred_element_type=F32)


def _dot_nt(a, b):
    return lax.dot_general(a.astype(BF16), b.astype(BF16), (((1,), (1,)), ((), ())),
                           preferred_element_type=F32)


def _split_bf16(x):
    hi = x.astype(BF16)
    return hi, (x - hi.astype(F32)).astype(BF16)


def _dot_left_exact(m, x):
    hi, lo = _split_bf16(x)
    return jnp.dot(m, hi, preferred_element_type=F32) + jnp.dot(m, lo, preferred_element_type=F32)


def _softplus(x):
    return jnp.maximum(x, 0.0) + jnp.log(1.0 + jnp.exp(-jnp.abs(x)))


def _ada_kernel(c_ref, w_ref, b_ref, o_ref):
    c = c_ref[...]
    o_ref[...] = _dot(c * jax.nn.sigmoid(c), w_ref[...]) + b_ref[...]


def ada_modulation(cvec, ada_w, ada_b):
    tn = 512
    return pl.pallas_call(
        _ada_kernel,
        grid=(DEPTH, 6 * D_MODEL // tn),
        in_specs=[pl.BlockSpec((8, D_MODEL), lambda l, j: (0, 0)),
                  pl.BlockSpec((None, D_MODEL, tn), lambda l, j: (l, 0, j)),
                  pl.BlockSpec((None, 1, tn), lambda l, j: (l, 0, j))],
        out_specs=pl.BlockSpec((None, 8, tn), lambda l, j: (l, 0, j)),
        out_shape=jax.ShapeDtypeStruct((DEPTH, 8, 6 * D_MODEL), F32),
        compiler_params=_params("arbitrary", "arbitrary"),
        name="ada_modulation",
    )(cvec, ada_w, ada_b.reshape(DEPTH, 1, 6 * D_MODEL))


def _norm_mod_kernel(x_ref, g_ref, sc_ref, sh_ref, o_ref):
    y = _rms(x_ref[...]) * g_ref[...]
    o_ref[...] = (y * (1.0 + sc_ref[...]) + sh_ref[...]).astype(o_ref.dtype)


def _mod_spec(l, which, tm):
    return pl.BlockSpec((None, None, None, 1, D_MODEL),
                        lambda i: (l, _group_of_tile(i, tm), which, 0, 0))


def _norm_spec(l, which):
    return pl.BlockSpec((None, None, 1, D_MODEL), lambda i: (l, which, 0, 0))


def norm_mod(x, norms, mod, l, norm_idx, sc_idx, sh_idx):
    tm = 256
    row = pl.BlockSpec((tm, D_MODEL), lambda i: (i, 0))
    return pl.pallas_call(
        _norm_mod_kernel,
        grid=(N_TOK // tm,),
        in_specs=[row, _norm_spec(l, norm_idx), _mod_spec(l, sc_idx, tm), _mod_spec(l, sh_idx, tm)],
        out_specs=row,
        out_shape=jax.ShapeDtypeStruct((N_TOK, D_MODEL), BF16),
        compiler_params=_params("arbitrary"),
        name="norm_mod",
    )(x, norms, mod, mod)


def _resid_norm_kernel(x_ref, y_ref, gy_ref, gate_ref, g2_ref, sc_ref, sh_ref, x1_ref, xn_ref):
    x1 = x_ref[...] + gate_ref[...] * (_rms(y_ref[...]) * gy_ref[...])
    x1_ref[...] = x1
    xn = _rms(x1) * g2_ref[...]
    xn_ref[...] = (xn * (1.0 + sc_ref[...]) + sh_ref[...]).astype(xn_ref.dtype)


def resid_norm(x, y, norms, mod, l, ynorm_idx, gate_idx, l2, norm2_idx, sc_idx, sh_idx):
    tm = 256
    row = pl.BlockSpec((tm, D_MODEL), lambda i: (i, 0))
    return pl.pallas_call(
        _resid_norm_kernel,
        grid=(N_TOK // tm,),
        in_specs=[row, row, _norm_spec(l, ynorm_idx), _mod_spec(l, gate_idx, tm),
                  _norm_spec(l2, norm2_idx), _mod_spec(l2, sc_idx, tm), _mod_spec(l2, sh_idx, tm)],
        out_specs=[row, row],
        out_shape=[jax.ShapeDtypeStruct((N_TOK, D_MODEL), F32),
                   jax.ShapeDtypeStruct((N_TOK, D_MODEL), F32)],
        compiler_params=_params("arbitrary"),
        name="resid_norm",
    )(x, y, norms, mod, norms, mod, mod)


def _mm_kernel(a_ref, w_ref, o_ref):
    o_ref[...] = _dot(a_ref[...], w_ref[...]).astype(o_ref.dtype)


def matmul_cols(a, w, lead, col_off, n_cols, out_dtype, name):
    m, k = a.shape
    tm, tn = 1024, 512
    assert m % tm == 0 and n_cols % tn == 0 and col_off % tn == 0
    nlead = len(lead)
    w_spec = pl.BlockSpec((None,) * nlead + (k, tn), lambda i, j: lead + (0, col_off // tn + j))
    return pl.pallas_call(
        _mm_kernel,
        grid=(m // tm, n_cols // tn),
        in_specs=[pl.BlockSpec((tm, k), lambda i, j: (i, 0)), w_spec],
        out_specs=pl.BlockSpec((tm, tn), lambda i, j: (i, j)),
        out_shape=jax.ShapeDtypeStruct((m, n_cols), out_dtype),
        compiler_params=_params("arbitrary", "arbitrary"),
        name=name,
    )(a, w)


def _gate_kernel(xn_ref, u_ref, wg_ref, wb_ref, o_ref, acc_ref):
    n = pl.program_id(2)
    val = jax.nn.sigmoid(_dot(xn_ref[...], wg_ref[...])) * _dot(u_ref[...], wb_ref[...])

    @pl.when(n == 0)
    def _():
        acc_ref[...] = val

    @pl.when(n > 0)
    def _():
        acc_ref[...] += val

    @pl.when(n == N_BRANCH - 1)
    def _():
        o_ref[...] = acc_ref[...].astype(o_ref.dtype)


def gated_branches(xn, u, w_rest, w_branch, l):
    tm, tn = 1024, 256
    return pl.pallas_call(
        _gate_kernel,
        grid=(N_TOK // tm, D_MODEL // tn, N_BRANCH),
        in_specs=[pl.BlockSpec((tm, D_MODEL), lambda i, j, n: (i, 0)),
                  pl.BlockSpec((tm, BRANCH_W), lambda i, j, n: (i, n)),
                  pl.BlockSpec((D_MODEL, tn), lambda i, j, n: (0, (LP_W + n * D_MODEL) // tn + j)),
                  pl.BlockSpec((None, None, BRANCH_W, tn), lambda i, j, n: (l, n, 0, j))],
        out_specs=pl.BlockSpec((tm, tn), lambda i, j, n: (i, j)),
        out_shape=jax.ShapeDtypeStruct((N_TOK, D_MODEL), BF16),
        scratch_shapes=[pltpu.VMEM((tm, tn), F32)],
        compiler_params=_params("arbitrary", "arbitrary", "arbitrary"),
        name="gated_branches",
    )(xn, u, w_rest, w_branch)


ROUTER_TM = 512


def _router_kernel(x_ref, w_ref, b_ref, tri_ref, idx_ref, gate_ref, rank_ref, cnt_ref, carry_ref):
    @pl.when(pl.program_id(0) == 0)
    def _():
        carry_ref[...] = jnp.zeros_like(carry_ref)

    logits = jnp.dot(x_ref[...].astype(BF16), w_ref[...], preferred_element_type=F32) + b_ref[...]
    lane = lax.broadcasted_iota(jnp.int32, logits.shape, 1).astype(F32)
    cur = jnp.where(lane < N_EXPERTS, logits, -jnp.inf)
    vals, idxs, hots = [], [], []
    for _ in range(TOP_K):
        m = jnp.max(cur, axis=-1, keepdims=True)
        idx = jnp.min(jnp.where(cur == m, lane, float(LANES)), axis=-1, keepdims=True)
        hot = lane == idx
        cur = jnp.where(hot, -jnp.inf, cur)
        vals.append(m)
        idxs.append(idx)
        hots.append(hot)
    exps = [jnp.exp(v - vals[0]) for v in vals]
    denom = exps[0] + exps[1] + exps[2] + exps[3]
    sel = jnp.zeros(logits.shape, F32)
    for hot in hots:
        sel = sel + hot.astype(F32)
    before = jnp.dot(tri_ref[...], sel.astype(BF16), preferred_element_type=F32) + carry_ref[...]
    idx_out = jnp.zeros(logits.shape, F32)
    gate_out = jnp.zeros(logits.shape, F32)
    rank_out = jnp.zeros(logits.shape, F32)
    for k in range(TOP_K):
        rank = jnp.sum(jnp.where(hots[k], before, 0.0), axis=-1, keepdims=True)
        idx_out = jnp.where(lane == k, idxs[k], idx_out)
        gate_out = jnp.where(lane == k, exps[k] / denom, gate_out)
        rank_out = jnp.where(lane == k, rank, rank_out)
    idx_ref[...] = idx_out.astype(jnp.int32)
    gate_ref[...] = gate_out
    rank_ref[...] = rank_out.astype(jnp.int32)
    carry_ref[...] += jnp.sum(sel, axis=0, keepdims=True)
    cnt_ref[...] = carry_ref[...].astype(jnp.int32)


def moe_route(xn2, router_w, router_b, l):
    tm = ROUTER_TM
    w = jnp.zeros((D_MODEL, LANES), BF16).at[:, :N_EXPERTS].set(router_w[l].astype(BF16))
    b = jnp.zeros((1, LANES), F32).at[0, :N_EXPERTS].set(router_b[l])
    tri = (lax.broadcasted_iota(jnp.int32, (tm, tm), 1)
           < lax.broadcasted_iota(jnp.int32, (tm, tm), 0)).astype(BF16)
    row = pl.BlockSpec((tm, LANES), lambda i: (i, 0))
    idx, gates, rank, cnt = pl.pallas_call(
        _router_kernel,
        grid=(N_TOK // tm,),
        in_specs=[pl.BlockSpec((tm, D_MODEL), lambda i: (i, 0)),
                  pl.BlockSpec((D_MODEL, LANES), lambda i: (0, 0)),
                  pl.BlockSpec((1, LANES), lambda i: (0, 0)),
                  pl.BlockSpec((tm, tm), lambda i: (0, 0))],
        out_specs=[row, row, row, pl.BlockSpec((1, LANES), lambda i: (0, 0))],
        out_shape=[jax.ShapeDtypeStruct((N_TOK, LANES), jnp.int32),
                   jax.ShapeDtypeStruct((N_TOK, LANES), F32),
                   jax.ShapeDtypeStruct((N_TOK, LANES), jnp.int32),
                   jax.ShapeDtypeStruct((1, LANES), jnp.int32)],
        scratch_shapes=[pltpu.VMEM((1, LANES), F32)],
        compiler_params=_params("arbitrary"),
        name="moe_route",
    )(xn2, w, b, tri)
    return idx[:, :TOP_K], gates[:, :TOP_K], rank[:, :TOP_K], cnt[0, :N_EXPERTS]


def _row_copy(src_hbm, dst_vmem, src_row, dst_row, sem):
    return pltpu.make_async_copy(src_hbm.at[pl.ds(src_row, 1), :], dst_vmem.at[pl.ds(dst_row, 1), :], sem)


def _gather_rows_kernel(tok_ref, nused_ref, x_hbm, o_ref, buf, sem):
    i = pl.program_id(0)
    base = i * MOE_ROWS_BLK

    @pl.when(i < nused_ref[0])
    def _():
        def start(r, c):
            _row_copy(x_hbm, buf, tok_ref[base + r], r, sem).start()
            return c

        def wait(r, c):
            _row_copy(x_hbm, buf, 0, r, sem).wait()
            return c

        lax.fori_loop(0, MOE_ROWS_BLK, start, 0, unroll=8)
        lax.fori_loop(0, MOE_ROWS_BLK, wait, 0, unroll=8)
        o_ref[...] = buf[...].astype(o_ref.dtype)

    @pl.when(i >= nused_ref[0])
    def _():
        o_ref[...] = jnp.zeros_like(o_ref)


def gather_rows(row_token, n_used, x):
    width = x.shape[1]
    return pl.pallas_call(
        _gather_rows_kernel,
        grid_spec=pltpu.PrefetchScalarGridSpec(
            num_scalar_prefetch=2,
            grid=(MOE_BLOCKS,),
            in_specs=[pl.BlockSpec(memory_space=pl.ANY)],
            out_specs=pl.BlockSpec((MOE_ROWS_BLK, width), lambda i, tok, nu: (i, 0)),
            scratch_shapes=[pltpu.VMEM((MOE_ROWS_BLK, width), x.dtype), pltpu.SemaphoreType.DMA(())]),
        out_shape=jax.ShapeDtypeStruct((MOE_ROWS, width), BF16),
        compiler_params=_params("arbitrary"),
        name="moe_gather_rows",
    )(row_token, n_used, x)


UP_TF = 512
DOWN_TN = 2048
N_ITEM_ARGS = 9


def _stream_expert_weights(q, items, copies, on_ready):
    ie, jt, _, first, _, nxt_e, nxt_t, has_nxt, slot = items

    @pl.when(q == 0)
    def _():
        for cp in copies(ie[0], jt[0], 0):
            cp.start()

    @pl.when(first[q] == 1)
    def _():
        s = slot[q]

        @pl.when(has_nxt[q] == 1)
        def _():
            for cp in copies(nxt_e[q], nxt_t[q], 1 - s):
                cp.start()

        for cp in copies(ie[q], jt[q], s):
            cp.wait()
        on_ready(s)


def _expert_up_kernel(*refs, l):
    items = refs[:N_ITEM_ARGS]
    x_ref, wg_hbm, wu_hbm, bg_ref, bu_ref, h_ref, wg_buf, wu_buf, wg_bf, wu_bf, sem = refs[N_ITEM_ARGS:]
    valid_ref = items[4]
    q = pl.program_id(0)

    def copies(e, tile, s):
        cols = pl.ds(tile * UP_TF, UP_TF)
        return (pltpu.make_async_copy(wg_hbm.at[l, e, :, cols], wg_buf.at[s], sem.at[0, s]),
                pltpu.make_async_copy(wu_hbm.at[l, e, :, cols], wu_buf.at[s], sem.at[1, s]))

    def to_bf16(s):
        wg_bf[...] = wg_buf[s].astype(BF16)
        wu_bf[...] = wu_buf[s].astype(BF16)

    _stream_expert_weights(q, items, copies, to_bf16)

    @pl.when(valid_ref[q] == 1)
    def _():
        x = x_ref[...]
        gate = jnp.dot(x, wg_bf[...], preferred_element_type=F32) + bg_ref[...]
        up = jnp.dot(x, wu_bf[...], preferred_element_type=F32) + bu_ref[...]
        gate = jnp.minimum(gate, SWIGLU_LIMIT)
        up = jnp.clip(up, -SWIGLU_LIMIT, SWIGLU_LIMIT)
        glu = gate * jax.nn.sigmoid(gate * SWIGLU_ALPHA)
        h_ref[...] = ((up + 1.0) * glu).astype(h_ref.dtype)

    @pl.when(valid_ref[q] == 0)
    def _():
        h_ref[...] = jnp.zeros_like(h_ref)


def expert_up(items, xs, w_gate, b_gate, w_up, b_up, l):
    tf = UP_TF
    n_items = items[0].shape[0]
    w_spec = pl.BlockSpec(memory_space=pl.ANY)
    b_spec = pl.BlockSpec((None, None, 1, tf), lambda q, *it: (l, it[0][q], 0, it[1][q]))
    return pl.pallas_call(
        functools.partial(_expert_up_kernel, l=l),
        grid_spec=pltpu.PrefetchScalarGridSpec(
            num_scalar_prefetch=N_ITEM_ARGS,
            grid=(n_items,),
            in_specs=[pl.BlockSpec((MOE_ROWS_BLK, D_MODEL), lambda q, *it: (it[2][q], 0)),
                      w_spec, w_spec, b_spec, b_spec],
            out_specs=pl.BlockSpec((MOE_ROWS_BLK, tf), lambda q, *it: (it[2][q], it[1][q])),
            scratch_shapes=[pltpu.VMEM((2, D_MODEL, tf), F32), pltpu.VMEM((2, D_MODEL, tf), F32),
                            pltpu.VMEM((D_MODEL, tf), BF16), pltpu.VMEM((D_MODEL, tf), BF16),
                            pltpu.SemaphoreType.DMA((2, 2))]),
        out_shape=jax.ShapeDtypeStruct((MOE_ROWS, EXPERT_FF), BF16),
        compiler_params=_params("arbitrary"),
        name="moe_expert_up",
    )(*items, xs, w_gate, w_up,
      b_gate.reshape(DEPTH, N_EXPERTS, 1, EXPERT_FF), b_up.reshape(DEPTH, N_EXPERTS, 1, EXPERT_FF))


def _expert_down_kernel(*refs, l):
    items = refs[:N_ITEM_ARGS]
    h_ref, wd_hbm, bd_ref, y_ref, wd_buf, wd_bf, sem = refs[N_ITEM_ARGS:]
    valid_ref = items[4]
    q = pl.program_id(0)

    def copies(e, tile, s):
        return (pltpu.make_async_copy(wd_hbm.at[l, e, :, pl.ds(tile * DOWN_TN, DOWN_TN)], wd_buf.at[s],
                                      sem.at[s]),)

    def to_bf16(s):
        wd_bf[...] = wd_buf[s].astype(BF16)

    _stream_expert_weights(q, items, copies, to_bf16)

    @pl.when(valid_ref[q] == 1)
    def _():
        y_ref[...] = jnp.dot(h_ref[...], wd_bf[...], preferred_element_type=F32) + bd_ref[...]

    @pl.when(valid_ref[q] == 0)
    def _():
        y_ref[...] = jnp.zeros_like(y_ref)


def expert_down(items, h, w_down, b_down, l):
    tn = DOWN_TN
    n_items = items[0].shape[0]
    return pl.pallas_call(
        functools.partial(_expert_down_kernel, l=l),
        grid_spec=pltpu.PrefetchScalarGridSpec(
            num_scalar_prefetch=N_ITEM_ARGS,
            grid=(n_items,),
            in_specs=[pl.BlockSpec((MOE_ROWS_BLK, EXPERT_FF), lambda q, *it: (it[2][q], 0)),
                      pl.BlockSpec(memory_space=pl.ANY),
                      pl.BlockSpec((None, None, 1, tn), lambda q, *it: (l, it[0][q], 0, it[1][q]))],
            out_specs=pl.BlockSpec((MOE_ROWS_BLK, tn), lambda q, *it: (it[2][q], it[1][q])),
            scratch_shapes=[pltpu.VMEM((2, EXPERT_FF, tn), F32), pltpu.VMEM((EXPERT_FF, tn), BF16),
                            pltpu.SemaphoreType.DMA((2,))]),
        out_shape=jax.ShapeDtypeStruct((MOE_ROWS, D_MODEL), F32),
        compiler_params=_params("arbitrary"),
        name="moe_expert_down",
    )(*items, h, w_down, b_down.reshape(DEPTH, N_EXPERTS, 1, D_MODEL))


COMBINE_TM = 128


def _combine_kernel(pos_ref, y_hbm, g_ref, x_ref, gy_ref, gate_ref, g2_ref, sc_ref, sh_ref,
                    x1_ref, xn_ref, buf, sem):
    base = pl.program_id(0) * (COMBINE_TM * TOP_K)

    def start(r, c):
        for k in range(TOP_K):
            _row_copy(y_hbm, buf.at[k], pos_ref[base + r * TOP_K + k], r, sem).start()
        return c

    def wait(r, c):
        for k in range(TOP_K):
            _row_copy(y_hbm, buf.at[k], 0, r, sem).wait()
        return c

    lax.fori_loop(0, COMBINE_TM, start, 0, unroll=4)
    lax.fori_loop(0, COMBINE_TM, wait, 0, unroll=4)
    g = g_ref[...]
    ff = buf[0] * g[:, 0:1]
    for k in range(1, TOP_K):
        ff = ff + buf[k] * g[:, k:k + 1]
    x1 = x_ref[...] + gate_ref[...] * (_rms(ff) * gy_ref[...])
    x1_ref[...] = x1
    xn = _rms(x1) * g2_ref[...]
    xn_ref[...] = (xn * (1.0 + sc_ref[...]) + sh_ref[...]).astype(xn_ref.dtype)


def moe_combine(pos_flat, yb, gates, x, norms, mod, l, l2):
    tm = COMBINE_TM
    row = pl.BlockSpec((tm, D_MODEL), lambda i, p: (i, 0))

    def mod_spec(ll, which):
        return pl.BlockSpec((None, None, None, 1, D_MODEL),
                            lambda i, p: (ll, _group_of_tile(i, tm), which, 0, 0))

    def norm_spec(ll, which):
        return pl.BlockSpec((None, None, 1, D_MODEL), lambda i, p: (ll, which, 0, 0))

    return pl.pallas_call(
        _combine_kernel,
        grid_spec=pltpu.PrefetchScalarGridSpec(
            num_scalar_prefetch=1,
            grid=(N_TOK // tm,),
            in_specs=[pl.BlockSpec(memory_space=pl.ANY),
                      pl.BlockSpec((tm, TOP_K), lambda i, p: (i, 0)),
                      row, norm_spec(l, 3), mod_spec(l, 5), norm_spec(l2, 0), mod_spec(l2, 1), mod_spec(l2, 0)],
            out_specs=[row, row],
            scratch_shapes=[pltpu.VMEM((TOP_K, tm, D_MODEL), F32), pltpu.SemaphoreType.DMA(())]),
        out_shape=[jax.ShapeDtypeStruct((N_TOK, D_MODEL), F32),
                   jax.ShapeDtypeStruct((N_TOK, D_MODEL), BF16)],
        compiler_params=_params("arbitrary"),
        name="moe_combine",
    )(pos_flat, yb, gates, x, norms, mod, norms, mod, mod)


def _work_items(n_blk_e, n_tiles):
    total = MOE_BLOCKS * n_tiles
    blk_end = jnp.cumsum(n_blk_e)
    blk_start = blk_end - n_blk_e
    n_used = blk_end[-1]
    used = n_used * n_tiles
    q = jnp.arange(total, dtype=jnp.int32)
    valid = q < used
    qc = jnp.minimum(q, used - 1)
    e = jnp.sum((qc[:, None] >= (blk_end * n_tiles)[None, :]).astype(jnp.int32), axis=1)
    rel = qc - blk_start[e] * n_tiles
    nb = jnp.maximum(n_blk_e[e], 1)
    n_tail = jnp.maximum(MOE_BLOCKS - n_used, 1)
    rel_tail = q - used
    tile = jnp.where(valid, rel // nb, rel_tail // n_tail)
    blk = jnp.where(valid, blk_start[e] + rel % nb, n_used + rel_tail % n_tail)
    first = (rel % nb == 0) & valid
    tile = tile.astype(jnp.int32)
    first_pos = jnp.where(first, q, total)
    nxt = lax.cummin(jnp.concatenate([first_pos[1:], jnp.full((1,), total, jnp.int32)]), reverse=True)
    has_nxt = nxt < total
    nxt = jnp.minimum(nxt, total - 1)
    slot = (jnp.cumsum(first.astype(jnp.int32)) - 1) % 2
    as_i32 = lambda t: t.astype(jnp.int32)
    return (e, tile, as_i32(blk), as_i32(first), as_i32(valid), e[nxt], tile[nxt], as_i32(has_nxt),
            as_i32(slot))


def moe_ffn(xn2, x1, norms, mod, l, l2, router_w, router_b, w_gate, b_gate, w_up, b_up, w_down, b_down):
    idx, gates, rank, counts = moe_route(xn2, router_w, router_b, l)
    n_blk_e = (counts + MOE_ROWS_BLK - 1) // MOE_ROWS_BLK
    pstart = (jnp.cumsum(n_blk_e) - n_blk_e) * MOE_ROWS_BLK
    pos = pstart[idx] + rank
    tok = jnp.broadcast_to(jnp.arange(N_TOK, dtype=jnp.int32)[:, None], pos.shape)
    row_token = jnp.zeros((MOE_ROWS,), jnp.int32).at[pos.reshape(-1)].set(tok.reshape(-1))
    xs = gather_rows(row_token, jnp.sum(n_blk_e).reshape(1).astype(jnp.int32), xn2)
    h = expert_up(_work_items(n_blk_e, EXPERT_FF // UP_TF), xs, w_gate, b_gate, w_up, b_up, l)
    yb = expert_down(_work_items(n_blk_e, D_MODEL // DOWN_TN), h, w_down, b_down, l)
    return moe_combine(pos.reshape(-1).astype(jnp.int32), yb, gates, x1, norms, mod, l, l2)


HALF = RWKV_HEAD
WKV_L = 64
SOLVE_BLK = 16
N_PAIR = BRANCH_W // LANES
MIX_TM = 256


def _head_sum(x):
    li = lax.broadcasted_iota(jnp.int32, (LANES, LANES), 0) // HALF
    lj = lax.broadcasted_iota(jnp.int32, (LANES, LANES), 1) // HALF
    ones_bd = (li == lj).astype(BF16)
    hi, lo = _split_bf16(x)
    parts = []
    for p in range(x.shape[1] // LANES):
        sl = slice(p * LANES, (p + 1) * LANES)
        parts.append(jnp.dot(hi[:, sl], ones_bd, preferred_element_type=F32)
                     + jnp.dot(lo[:, sl], ones_bd, preferred_element_type=F32))
    return jnp.concatenate(parts, axis=1)


def _rwkv_prep_kernel(z_ref, zp_ref, zn_ref, vf_ref, mu_ref, w0_ref, w2_ref, a0_ref, a2_ref, kk_ref, ka_ref,
                      rk_ref, g2_ref, v0_ref, v1_ref, v2_ref,
                      r_o, v_o, kk_o, g_o, bonus_o, lwf_o, kf_o, bf_o, lwb_o, kb_o, bb_o,
                      *, n_ctx_blocks, lat_blocks, has_vres):
    i = pl.program_id(0)
    j = jnp.maximum(i - n_ctx_blocks, 0) % lat_blocks
    is_lat = i >= n_ctx_blocks
    has_prev = jnp.logical_and(is_lat, j > 0).astype(F32)
    has_next = jnp.logical_and(is_lat, j < lat_blocks - 1).astype(F32)
    z = z_ref[...]
    tm = z.shape[0]
    row = lax.broadcasted_iota(jnp.int32, z.shape, 0)
    prev_row = zp_ref[7:8, :] * has_prev
    next_row = zn_ref[0:1, :] * has_next
    z_up = jnp.where(row == 0, prev_row, pltpu.roll(z, 1, 0))
    z_dn = jnp.where(row == tm - 1, next_row, pltpu.roll(z, tm - 1, 0))
    z = z + mu_ref[...] * (0.5 * (z_up + z_dn) - z)

    C = BRANCH_W
    r, k, v = z[:, 0:C], z[:, C:2 * C], z[:, 2 * C:3 * C]
    lora = z[:, 3 * C:3 * C + 2 * LANES]
    gd = z[:, 3 * C + 2 * LANES:3 * C + 4 * LANES]
    if has_vres:
        mixv = jax.nn.sigmoid(v0_ref[...] + _dot(_dot(v, v1_ref[...]), v2_ref[...]))
        v = v + (vf_ref[...] - v) * mixv
    g = _dot(jax.nn.sigmoid(gd), g2_ref[...])
    kk = k * kk_ref[...]
    kk = kk / jnp.maximum(jnp.sqrt(_head_sum(kk * kk)), 1e-12)
    r_o[...] = r
    v_o[...] = v
    kk_o[...] = kk
    g_o[...] = g
    bonus_o[...] = _head_sum(r * k * rk_ref[...]) * v
    tw = jnp.tanh(lora[:, :LANES])
    ad = lora[:, LANES:]
    for d, (lw_o, k_o, b_o) in enumerate(((lwf_o, kf_o, bf_o), (lwb_o, kb_o, bb_o))):
        w_log = -_softplus(-(w0_ref[d:d + 1, :] + _dot(tw, w2_ref[d]))) - 0.5
        lw_o[...] = -jnp.exp(w_log)
        rate = jax.nn.sigmoid(a0_ref[d:d + 1, :] + _dot(ad, a2_ref[d]))
        k_o[...] = k * (1.0 + (rate - 1.0) * ka_ref[d:d + 1, :])
        b_o[...] = kk * rate


def rwkv_prep(z_r, v_first, l, n_ctx_blocks, lat_blocks, rwkv_mu, rwkv_w0, rwkv_w2, rwkv_a0, rwkv_a2,
              rwkv_kk, rwkv_ka, rwkv_rk, rwkv_g2, rwkv_v0, rwkv_v1, rwkv_v2):
    n = z_r.shape[0]
    tm = MIX_TM
    has_vres = l > 0
    C = BRANCH_W
    mu = jnp.zeros((1, RWKV_PAD), F32).at[0, :RWKV_IN].set(rwkv_mu[l])
    w2 = jnp.zeros((2, LANES, C), F32)
    a2 = jnp.zeros((2, LANES, C), F32)
    for d in range(2):
        w2 = w2.at[d, d * RWKV_W_RANK:(d + 1) * RWKV_W_RANK].set(rwkv_w2[l, d])
        a2 = a2.at[d, d * RWKV_A_RANK:(d + 1) * RWKV_A_RANK].set(rwkv_a2[l, d])
    g2 = jnp.zeros((2 * LANES, C), F32).at[:RWKV_G_RANK].set(rwkv_g2[l])
    lv = max(l - 1, 0)
    v0 = rwkv_v0[lv].reshape(1, C)
    v1 = jnp.zeros((C, LANES), F32).at[:, :RWKV_V_RANK].set(rwkv_v1[lv])
    v2 = jnp.zeros((LANES, C), F32).at[:RWKV_V_RANK].set(rwkv_v2[lv])
    if v_first is None:
        v_first = jnp.zeros((8, C), F32)
        vf_spec = pl.BlockSpec((8, C), lambda i: (0, 0))
    else:
        vf_spec = pl.BlockSpec((tm, C), lambda i: (i, 0))
    halo = tm // 8
    last8 = n // 8 - 1

    def full(shape):
        return pl.BlockSpec(shape, lambda i: (0,) * len(shape))

    out_row = pl.BlockSpec((tm, C), lambda i: (i, 0))
    return pl.pallas_call(
        functools.partial(_rwkv_prep_kernel, n_ctx_blocks=n_ctx_blocks, lat_blocks=lat_blocks,
                          has_vres=has_vres),
        grid=(n // tm,),
        in_specs=[pl.BlockSpec((tm, RWKV_PAD), lambda i: (i, 0)),
                  pl.BlockSpec((8, RWKV_PAD), lambda i: (jnp.maximum(i * halo - 1, 0), 0)),
                  pl.BlockSpec((8, RWKV_PAD), lambda i: (jnp.minimum((i + 1) * halo, last8), 0)),
                  vf_spec, full((1, RWKV_PAD)), full((2, C)), full((2, LANES, C)), full((2, C)),
                  full((2, LANES, C)), full((1, C)), full((2, C)), full((1, C)), full((2 * LANES, C)),
                  full((1, C)), full((C, LANES)), full((LANES, C))],
        out_specs=[out_row] * 11,
        out_shape=[jax.ShapeDtypeStruct((n, C), F32)] * 11,
        compiler_params=_params("arbitrary"),
        name="rwkv_prep",
    )(z_r, z_r, z_r, v_first, mu, rwkv_w0[l], w2, rwkv_a0[l], a2, rwkv_kk[l].reshape(1, C), rwkv_ka[l],
      rwkv_rk[l].reshape(1, C), g2, v0, v1, v2)


def _bd(x):
    lane = lax.broadcasted_iota(jnp.int32, x.shape, 1)
    return jnp.concatenate([jnp.where(lane < HALF, x, 0.0), jnp.where(lane >= HALF, x, 0.0)], axis=0)


def _dot_hp(a, b):
    a_hi, a_lo = _split_bf16(a)
    b_hi, b_lo = _split_bf16(b)
    bd_hi = _bd(b_hi)
    return (jnp.dot(a_hi, bd_hi, preferred_element_type=F32)
            + jnp.dot(a_lo, bd_hi, preferred_element_type=F32)
            + jnp.dot(a_hi, _bd(b_lo), preferred_element_type=F32))


def _wkv_direction(r, v, kk, lw, k, b, s_ref, d, rev, y_ref):
    L = WKV_L
    ti = lax.broadcasted_iota(jnp.int32, (L, L), 0)
    si = lax.broadcasted_iota(jnp.int32, (L, L), 1)
    tri = ((si >= ti) if rev else (si <= ti)).astype(BF16)
    cs = _dot_left_exact(tri, lw)
    ctot = cs[0:1] if rev else cs[L - 1:L]
    e_neg = jnp.exp(-cs)
    rt = r * jnp.exp(cs)
    at = -kk * jnp.exp(cs - lw)
    bt = b * e_neg
    kt = k * e_neg
    e_rem = jnp.exp(ctot - cs)
    bh = b * e_rem
    kh = k * e_rem
    wtot = jnp.exp(ctot)
    row = lax.broadcasted_iota(jnp.int32, (L, LANES), 0)
    sidx = lax.broadcasted_iota(jnp.int32, (L, LANES), 1) % HALF
    strict = (sidx > row) if rev else (sidx < row)
    incl = (sidx >= row) if rev else (sidx <= row)
    eye = (sidx == row).astype(F32)
    same_head = (lax.broadcasted_iota(jnp.int32, (LANES, LANES), 0) // HALF
                 == lax.broadcasted_iota(jnp.int32, (LANES, LANES), 1) // HALF)
    pairs = range(N_PAIR)
    sls = [slice(p * LANES, (p + 1) * LANES) for p in pairs]
    S = [s_ref[d, p] for p in pairs]
    ar = [jnp.concatenate([at[:, sl], rt[:, sl]], axis=0) for sl in sls]
    g_b = [_dot_nt(ar[p], _bd(bt[:, sls[p]])) for p in pairs]
    g_k = [_dot_nt(ar[p], _bd(kt[:, sls[p]])) for p in pairs]
    g_s = [_dot_nt(ar[p], S[p]) for p in pairs]
    n_ab = [jnp.where(strict, g_b[p][:L], 0.0) for p in pairs]
    same_blk = (sidx // SOLVE_BLK) == (row // SOLVE_BLK)
    pw = [jnp.where(same_blk, n_ab[p], 0.0) for p in pairs]
    dinv = [eye + pw[p] for p in pairs]
    step = 2
    while step < SOLVE_BLK:
        pw = [_dot_hp(pw[p], pw[p]) for p in pairs]
        dinv = [dinv[p] + _dot_hp(dinv[p], pw[p]) for p in pairs]
        step *= 2
    vb = [_bd(v[:, sl]) for sl in sls]
    l_ak = [jnp.where(strict, g_k[p][:L], 0.0) for p in pairs]
    rhs = [g_s[p][:L] + _dot(l_ak[p], vb[p]) for p in pairs]
    n_blk = L // SOLVE_BLK
    zeros_blk = jnp.zeros((SOLVE_BLK, LANES), F32)
    done = [[None] * n_blk for _ in pairs]

    def rows_with(blocks):
        return jnp.concatenate([zeros_blk if b is None else b for b in blocks], axis=0)

    for kb in (range(n_blk - 1, -1, -1) if rev else range(n_blk)):
        rows = slice(kb * SOLVE_BLK, (kb + 1) * SOLVE_BLK)
        for p in pairs:
            acc = rhs[p][rows]
            if any(b is not None for b in done[p]):
                acc = acc + _dot(n_ab[p][rows], _bd(rows_with(done[p])))
            only = [None] * n_blk
            only[kb] = acc
            done[p][kb] = _dot(dinv[p][rows], _bd(rows_with(only)))
    ut = [rows_with(done[p]) for p in pairs]
    for p in pairs:
        l_rb = jnp.where(incl, g_b[p][L:], 0.0)
        l_rk = jnp.where(incl, g_k[p][L:], 0.0)
        y_ref[:, sls[p]] = g_s[p][L:] + _dot(l_rb, _bd(ut[p])) + _dot(l_rk, vb[p])
    for p in pairs:
        upd = _dot(jnp.concatenate([ut[p], v[:, sls[p]]], axis=0).T,
                   jnp.concatenate([bh[:, sls[p]], kh[:, sls[p]]], axis=0))
        s_ref[d, p] = S[p] * wtot[:, sls[p]] + jnp.where(same_head, upd, 0.0)


def _wkv_kernel(rf_ref, vf_ref, kkf_ref, lwf_ref, kf_ref, bf_ref, rb_ref, vb_ref, kkb_ref, lwb_ref, kb_ref,
                bb_ref, s0_ref, yf_ref, yb_ref, st_ref, s_scr):
    c = pl.program_id(1)

    @pl.when(c == 0)
    def _():
        s_scr[...] = s0_ref[...]

    _wkv_direction(rf_ref[...], vf_ref[...], kkf_ref[...], lwf_ref[...], kf_ref[...], bf_ref[...],
                   s_scr, 0, False, yf_ref)
    _wkv_direction(rb_ref[...], vb_ref[...], kkb_ref[...], lwb_ref[...], kb_ref[...], bb_ref[...],
                   s_scr, 1, True, yb_ref)

    @pl.when(c == pl.num_programs(1) - 1)
    def _():
        st_ref[...] = s_scr[...]


def wkv_scan(prep, s0_bd, row_off, n_seq, t_len):
    r, v, kk, lwf, kf, bf, lwb, kb, bb = prep
    L = WKV_L
    nc = t_len // L
    base = row_off // L
    rows = n_seq * t_len
    fwd = pl.BlockSpec((L, BRANCH_W), lambda s, c: (base + s * nc + c, 0))
    bwd = pl.BlockSpec((L, BRANCH_W), lambda s, c: (base + s * nc + (nc - 1 - c), 0))
    y_fwd = pl.BlockSpec((L, BRANCH_W), lambda s, c: (s * nc + c, 0))
    y_bwd = pl.BlockSpec((L, BRANCH_W), lambda s, c: (s * nc + (nc - 1 - c), 0))
    st = pl.BlockSpec((None, 2, N_PAIR, LANES, LANES), lambda s, c: (s, 0, 0, 0, 0))
    return pl.pallas_call(
        _wkv_kernel,
        grid=(n_seq, nc),
        in_specs=[fwd] * 6 + [bwd] * 6 + [st],
        out_specs=[y_fwd, y_bwd, st],
        out_shape=[jax.ShapeDtypeStruct((rows, BRANCH_W), F32), jax.ShapeDtypeStruct((rows, BRANCH_W), F32),
                   jax.ShapeDtypeStruct((n_seq, 2, N_PAIR, LANES, LANES), F32)],
        scratch_shapes=[pltpu.VMEM((2, N_PAIR, LANES, LANES), F32)],
        compiler_params=_params("arbitrary", "arbitrary"),
        name="wkv_scan",
    )(r, v, kk, lwf, kf, bf, r, v, kk, lwb, kb, bb, s0_bd)


def pair_states(S):
    out = jnp.zeros((S.shape[0], 2, N_PAIR, LANES, LANES), F32)
    out = out.at[:, :, :, :HALF, :HALF].set(S[:, :, 0::2])
    return out.at[:, :, :, HALF:, HALF:].set(S[:, :, 1::2])


def unpair_states(Sp):
    both = jnp.stack([Sp[:, :, :, :HALF, :HALF], Sp[:, :, :, HALF:, HALF:]], axis=3)
    return both.reshape(Sp.shape[0], 2, RWKV_H, RWKV_HEAD, RWKV_HEAD)


def _rwkv_post_kernel(yf_ref, yb_ref, bonus_ref, g_ref, gw_ref, gb_ref, o_ref):
    y = yf_ref[...] + yb_ref[...]
    inv_n = 1.0 / RWKV_HEAD
    d = y - _head_sum(y) * inv_n
    var = _head_sum(d * d) * inv_n
    yn = d * lax.rsqrt(var + RWKV_GN_EPS) * gw_ref[...] + gb_ref[...]
    o_ref[...] = ((yn + bonus_ref[...]) * g_ref[...]).astype(o_ref.dtype)


def rwkv_post(yf, yb, bonus, g, gn_w, gn_b):
    n = yf.shape[0]
    tm = MIX_TM
    C = BRANCH_W
    row = pl.BlockSpec((tm, C), lambda i: (i, 0))
    vec = pl.BlockSpec((1, C), lambda i: (0, 0))
    return pl.pallas_call(
        _rwkv_post_kernel,
        grid=(n // tm,),
        in_specs=[row, row, row, row, vec, vec],
        out_specs=row,
        out_shape=jax.ShapeDtypeStruct((n, C), BF16),
        compiler_params=_params("arbitrary"),
        name="rwkv_post",
    )(yf, yb, bonus, g, gn_w.reshape(1, C), gn_b.reshape(1, C))


def _expm1(x):
    small = x * (1.0 + x * (0.5 + x * (1.0 / 6.0 + x * (1.0 / 24.0 + x * (1.0 / 120.0)))))
    return jnp.where(jnp.abs(x) < 0.1, small, jnp.exp(x) - 1.0)


def _gelu_tanh(x):
    return 0.5 * x * (1.0 + jnp.tanh(0.7978845608028654 * (x + 0.044715 * x * x * x)))


def _lru_kernel(zx_ref, zg_ref, h0_ref, cw_ref, cb_ref, wa_ref, ba_ref, wx_ref, bx_ref, lam_ref,
                y_ref, hT_ref, af, hf, ab, hb):
    T = zx_ref.shape[0]
    x = zx_ref[...]
    row = lax.broadcasted_iota(jnp.int32, x.shape, 0)
    xc = cb_ref[...] + x * cw_ref[1:2, :]
    xc = xc + jnp.where(row >= 1, pltpu.roll(x, 1, 0), 0.0) * cw_ref[0:1, :]
    xc = xc + jnp.where(row < T - 1, pltpu.roll(x, T - 1, 0), 0.0) * cw_ref[2:3, :]
    xc = xc + jnp.where(row < T - 2, pltpu.roll(x, T - 2, 0), 0.0) * cw_ref[3:4, :]
    for d, (a_s, h_s) in enumerate(((af, hf), (ab, hb))):
        rg = jax.nn.sigmoid(_dot(xc, wa_ref[d]) + ba_ref[d:d + 1, :])
        ig = jax.nn.sigmoid(_dot(xc, wx_ref[d]) + bx_ref[d:d + 1, :])
        log_a = -LRU_C * rg * _softplus(-lam_ref[d:d + 1, :])
        a_s[...] = jnp.exp(log_a)
        h_s[...] = jnp.sqrt(-_expm1(2.0 * log_a)) * (ig * xc)

    def step(t, carry):
        h_f, h_b = carry
        tb = T - 1 - t
        h_f = af[pl.ds(t, 1), :] * h_f + hf[pl.ds(t, 1), :]
        h_b = ab[pl.ds(tb, 1), :] * h_b + hb[pl.ds(tb, 1), :]
        hf[pl.ds(t, 1), :] = h_f
        hb[pl.ds(tb, 1), :] = h_b
        return h_f, h_b

    h_f, h_b = lax.fori_loop(0, T, step, (h0_ref[0:1, :], h0_ref[1:2, :]), unroll=8)
    hT_ref[0:1, :] = h_f
    hT_ref[1:2, :] = h_b
    y_ref[...] = ((hf[...] + hb[...]) * _gelu_tanh(zg_ref[...])).astype(y_ref.dtype)


def rglru_mixer(z_lp, h0, row_off, n_seq, t_len, l, lru_conv_w, lru_conv_b, lru_wa, lru_ba, lru_wx, lru_bx,
                lru_lam):
    W, BW = LRU_W, LRU_BW
    base = row_off // t_len
    st = pl.BlockSpec((None, 2, BW), lambda s, b: (s, 0, b))
    vec2 = pl.BlockSpec((None, 2, BW), lambda s, b: (l, 0, b))
    mat = pl.BlockSpec((None, 2, None, BW, BW), lambda s, b: (l, 0, b, 0, 0))
    return pl.pallas_call(
        _lru_kernel,
        grid=(n_seq, LRU_BLOCKS),
        in_specs=[pl.BlockSpec((t_len, BW), lambda s, b: (base + s, b)),
                  pl.BlockSpec((t_len, BW), lambda s, b: (base + s, LRU_BLOCKS + b)),
                  st,
                  pl.BlockSpec((None, LRU_CONV, BW), lambda s, b: (l, 0, b)),
                  pl.BlockSpec((None, 1, BW), lambda s, b: (l, 0, b)),
                  mat, vec2, mat, vec2, vec2],
        out_specs=[pl.BlockSpec((t_len, BW), lambda s, b: (s, b)), st],
        out_shape=[jax.ShapeDtypeStruct((n_seq * t_len, W), BF16), jax.ShapeDtypeStruct((n_seq, 2, W), F32)],
        scratch_shapes=[pltpu.VMEM((t_len, BW), F32)] * 4,
        compiler_params=_params("arbitrary", "arbitrary"),
        name="rglru_mixer",
    )(z_lp, z_lp, h0, lru_conv_w, lru_conv_b.reshape(DEPTH, 1, W), lru_wa, lru_ba, lru_wx, lru_bx, lru_lam)


def _fnet_kernel(z_ref, ct_ref, st_ref, cc_ref, sc_ref, w_ref, o_ref):
    z = z_ref[...]
    T = z.shape[0]
    zc = _dot(z, cc_ref[...])
    zs = _dot(z, sc_ref[...])
    f = (_dot(ct_ref[...], zc) - _dot(st_ref[...], zs)) * (1.0 / (T * FNET_GW) ** 0.5)
    o_ref[...] = _dot(f, w_ref[...]).astype(o_ref.dtype)


def _dft(n):
    idx = jnp.arange(n, dtype=jnp.int32)
    ang = ((idx[:, None] * idx[None, :]) % n).astype(F32) * (2.0 * jnp.pi / n)
    return jnp.cos(ang).astype(BF16), jnp.sin(ang).astype(BF16)


def fourier_mixer(z_f, row_off, n_seq, t_len, l, fnet_w):
    base = row_off // t_len
    ct, st = _dft(t_len)
    cc, sc = _dft(FNET_GW)
    sq = lambda n: pl.BlockSpec((n, n), lambda s, g: (0, 0))
    return pl.pallas_call(
        _fnet_kernel,
        grid=(n_seq, FNET_GROUPS),
        in_specs=[pl.BlockSpec((t_len, FNET_GW), lambda s, g: (base + s, g)),
                  sq(t_len), sq(t_len), sq(FNET_GW), sq(FNET_GW),
                  pl.BlockSpec((None, None, FNET_GW, FNET_GW), lambda s, g: (l, g, 0, 0))],
        out_specs=pl.BlockSpec((t_len, FNET_GW), lambda s, g: (s, g)),
        out_shape=jax.ShapeDtypeStruct((n_seq * t_len, BRANCH_W), BF16),
        compiler_params=_params("arbitrary", "arbitrary"),
        name="fourier_mixer",
    )(z_f, ct, st, cc, sc, fnet_w)


def _pool_kernel(z_ref, w_ref, sc_ref, o_ref):
    z = z_ref[...]
    T = z.shape[0]
    half = jnp.left_shift(1, pl.program_id(1))
    ti = lax.broadcasted_iota(jnp.int32, (T, T), 0)
    si = lax.broadcasted_iota(jnp.int32, (T, T), 1)
    band = jnp.logical_and(si >= ti - half, si < ti + half).astype(BF16)
    t1 = lax.broadcasted_iota(jnp.int32, (T, 1), 0)
    cnt = (jnp.minimum(t1 + half, T) - jnp.maximum(t1 - half, 0)).astype(F32)
    d = _dot_left_exact(band, z) / cnt - z
    o_ref[...] = (_dot(d, w_ref[...]) * sc_ref[...]).astype(o_ref.dtype)


def pool_mixer(z_lp, row_off, n_seq, t_len, l, pool_w, pool_scale):
    base = row_off // t_len
    col0 = 2 * LRU_W // POOL_GW
    return pl.pallas_call(
        _pool_kernel,
        grid=(n_seq, len(POOL_WINDOWS)),
        in_specs=[pl.BlockSpec((t_len, POOL_GW), lambda s, g: (base + s, col0 + g)),
                  pl.BlockSpec((None, None, POOL_GW, POOL_GW), lambda s, g: (l, g, 0, 0)),
                  pl.BlockSpec((None, 1, POOL_GW), lambda s, g: (l, 0, g))],
        out_specs=pl.BlockSpec((t_len, POOL_GW), lambda s, g: (s, g)),
        out_shape=jax.ShapeDtypeStruct((n_seq * t_len, BRANCH_W), BF16),
        compiler_params=_params("arbitrary", "arbitrary"),
        name="pool_mixer",
    )(z_lp, pool_w, pool_scale.reshape(DEPTH, 1, BRANCH_W))


def _grid_position_code(n_tokens):
    rows = n_tokens // GRID_W
    rr, cc = jnp.meshgrid(jnp.arange(rows, dtype=F32), jnp.arange(GRID_W, dtype=F32), indexing="ij")
    quarter = D_MODEL // 4
    omega = POS_BASE ** (-jnp.arange(quarter, dtype=F32) / quarter)

    def enc(pos):
        ang = pos.reshape(-1)[:, None] * omega[None, :]
        return jnp.concatenate([jnp.sin(ang), jnp.cos(ang)], axis=-1)

    return jnp.concatenate([enc(rr), enc(cc)], axis=-1)


def kernel(x_prompt, x_sample, state_rwkv, state_lru, c, c_ctx, ada_w, ada_b, norms, w_in, fnet_w, rwkv_mu, rwkv_w0, rwkv_w2, rwkv_a0, rwkv_a2, rwkv_kk, rwkv_ka, rwkv_rk, rwkv_g2, rwkv_gn_w, rwkv_gn_b, rwkv_v0, rwkv_v1, rwkv_v2, lru_conv_w, lru_conv_b, lru_wa, lru_ba, lru_wx, lru_bx, lru_lam, pool_w, pool_scale, w_branch, w_out, router_w, router_b, moe_w_gate, moe_b_gate, moe_w_up, moe_b_up, moe_w_down, moe_b_down):
    x_lat = x_sample + _grid_position_code(DEC_SEQ).astype(x_sample.dtype)[None]
    x = jnp.concatenate([x_prompt.reshape(N_CTX, D_MODEL), x_lat.reshape(N_LAT, D_MODEL)], axis=0)

    cvec = jnp.zeros((8, D_MODEL), F32).at[0].set(c_ctx).at[1:1 + DEC_BATCH].set(c)
    mod = ada_modulation(cvec, ada_w, ada_b)[:, :N_GROUPS].reshape(DEPTH, N_GROUPS, 6, 1, D_MODEL)
    norms4 = norms.reshape(DEPTH, 4, 1, D_MODEL)

    groups = ((0, BATCH, SEQ), (N_CTX, DEC_BATCH, DEC_SEQ))
    s0_ctx = jnp.zeros((BATCH, 2, N_PAIR, LANES, LANES), F32)
    h0_ctx = jnp.zeros((BATCH, 2, LRU_W), F32)
    v_first = None
    S_ctx, h_ctx = [], []

    xn = norm_mod(x, norms4, mod, 0, 0, 1, 0)
    for l in range(DEPTH):
        z_f = matmul_cols(xn, w_in, (l,), 0, BRANCH_W, F32, "in_proj_fourier")
        z_r = matmul_cols(xn, w_in, (l,), BRANCH_W, RWKV_PAD, F32, "in_proj_rwkv")
        w_rest = w_in[l, :, REST_OFF:]
        z_lp = matmul_cols(xn, w_rest, (), 0, LP_W, F32, "in_proj_lru_pool")

        r, v, kk, g, bonus, lwf, kf, bf, lwb, kb, bb = rwkv_prep(
            z_r, v_first, l, N_CTX // MIX_TM, DEC_SEQ // MIX_TM, rwkv_mu, rwkv_w0, rwkv_w2, rwkv_a0, rwkv_a2,
            rwkv_kk, rwkv_ka, rwkv_rk, rwkv_g2, rwkv_v0, rwkv_v1, rwkv_v2)
        if l == 0:
            v_first = v
        scan_in = (r, v, kk, lwf, kf, bf, lwb, kb, bb)
        s0 = (s0_ctx, pair_states(state_rwkv[:, l]))
        h0 = (h0_ctx, state_lru[:, l])
        yf, yb, u_f, u_l, u_p = [], [], [], [], []
        for gi, (row0, n_seq, t_len) in enumerate(groups):
            yf_g, yb_g, s_g = wkv_scan(scan_in, s0[gi], row0, n_seq, t_len)
            yf.append(yf_g)
            yb.append(yb_g)
            u_f.append(fourier_mixer(z_f, row0, n_seq, t_len, l, fnet_w))
            u_l_g, h_g = rglru_mixer(z_lp, h0[gi], row0, n_seq, t_len, l, lru_conv_w, lru_conv_b, lru_wa,
                                     lru_ba, lru_wx, lru_bx, lru_lam)
            u_l.append(u_l_g)
            u_p.append(pool_mixer(z_lp, row0, n_seq, t_len, l, pool_w, pool_scale))
            if gi == 0:
                S_ctx.append(unpair_states(s_g))
                h_ctx.append(h_g)
        cat = lambda parts: jnp.concatenate(parts, axis=0)
        u_r = rwkv_post(cat(yf), cat(yb), bonus, g, rwkv_gn_w[l], rwkv_gn_b[l])
        u = jnp.concatenate([cat(u_f), u_r, cat(u_l), cat(u_p)], axis=1)
        gated = gated_branches(xn, u, w_rest, w_branch, l)
        mix = matmul_cols(gated, w_out, (l,), 0, D_MODEL, F32, "out_proj")
        x1, xn2 = resid_norm(x, mix, norms4, mod, l, 1, 2, l, 2, 4, 3)
        l2 = min(l + 1, DEPTH - 1)
        x, xn = moe_ffn(xn2, x1, norms4, mod, l, l2, router_w, router_b, moe_w_gate, moe_b_gate,
                        moe_w_up, moe_b_up, moe_w_down, moe_b_down)

    y_prompt = x[:N_CTX].reshape(BATCH, SEQ, D_MODEL)
    y_sample = x[N_CTX:].reshape(DEC_BATCH, DEC_SEQ, D_MODEL)
    return (y_prompt, y_sample, jnp.stack(S_ctx, axis=1), jnp.stack(h_ctx, axis=1))
```

```python
import functools

import jax
import jax.numpy as jnp
from jax import lax
from jax.experimental import pallas as pl
from jax.experimental.pallas import tpu as pltpu

D_MODEL = 4096
BATCH = 32
SEQ = 256
DEPTH = 2
DEC_BATCH = 2
DEC_SEQ = 1024

GRID_W = 64
N_BRANCH = 4
BRANCH_W = D_MODEL // 4
FNET_GROUPS = 4
FNET_GW = BRANCH_W // FNET_GROUPS
RWKV_HEAD = 64
RWKV_H = BRANCH_W // RWKV_HEAD
RWKV_W_RANK = 64
RWKV_A_RANK = 64
RWKV_V_RANK = 32
RWKV_G_RANK = 160
RWKV_GN_EPS = 64e-5
RWKV_IN = 3 * BRANCH_W + 2 * RWKV_W_RANK + 2 * RWKV_A_RANK + RWKV_G_RANK
LRU_W = BRANCH_W
LRU_BLOCKS = 4
LRU_BW = LRU_W // LRU_BLOCKS
LRU_CONV = 4
LRU_C = 8.0
POOL_WINDOWS = (2, 4, 8, 16)
POOL_GW = BRANCH_W // len(POOL_WINDOWS)
P_IN = BRANCH_W + RWKV_IN + 2 * LRU_W + BRANCH_W + N_BRANCH * D_MODEL
N_EXPERTS = 32
TOP_K = 4
EXPERT_FF = D_MODEL // 2
SWIGLU_LIMIT = 7.0
SWIGLU_ALPHA = 1.702
RMS_EPS = 1e-6
POS_BASE = 10000.0

LANES = 128
N_CTX = BATCH * SEQ
N_LAT = DEC_BATCH * DEC_SEQ
N_TOK = N_CTX + N_LAT
N_GROUPS = 1 + DEC_BATCH
RWKV_PAD = -(-RWKV_IN // LANES) * LANES
REST_OFF = BRANCH_W + RWKV_IN
LP_W = 2 * LRU_W + BRANCH_W
MOE_ROWS_BLK = 256
MOE_BLOCKS = -(-(N_TOK * TOP_K) // MOE_ROWS_BLK) + N_EXPERTS
MOE_ROWS = MOE_BLOCKS * MOE_ROWS_BLK
VMEM_LIMIT = 56 * 1024 * 1024

F32 = jnp.float32
BF16 = jnp.bfloat16


def _params(*sem):
    return pltpu.CompilerParams(dimension_semantics=sem, vmem_limit_bytes=VMEM_LIMIT)


def _group_of_tile(i, tm):
    n_ctx_tiles = N_CTX // tm
    return jnp.where(i < n_ctx_tiles, 0, 1 + (i - n_ctx_tiles) // (DEC_SEQ // tm))


def _rms(x):
    return x * lax.rsqrt(jnp.mean(x * x, axis=-1, keepdims=True) + RMS_EPS)


def _dot(a, b):
    return jnp.dot(a.astype(BF16), b.astype(BF16), preferred_element_type=F32)


def _dot_nt(a, b):
    return lax.dot_general(a.astype(BF16), b.astype(BF16), (((1,), (1,)), ((), ())),
                           preferred_element_type=F32)


def _split_bf16(x):
    hi = x.astype(BF16)
    return hi, (x - hi.astype(F32)).astype(BF16)


def _dot_left_exact(m, x):
    hi, lo = _split_bf16(x)
    return jnp.dot(m, hi, preferred_element_type=F32) + jnp.dot(m, lo, preferred_element_type=F32)


def _softplus(x):
    return jnp.maximum(x, 0.0) + jnp.log(1.0 + jnp.exp(-jnp.abs(x)))


def _ada_kernel(c_ref, w_ref, b_ref, o_ref):
    c = c_ref[...]
    o_ref[...] = _dot(c * jax.nn.sigmoid(c), w_ref[...]) + b_ref[...]


def ada_modulation(cvec, ada_w, ada_b):
    tn = 512
    return pl.pallas_call(
        _ada_kernel,
        grid=(DEPTH, 6 * D_MODEL // tn),
        in_specs=[pl.BlockSpec((8, D_MODEL), lambda l, j: (0, 0)),
                  pl.BlockSpec((None, D_MODEL, tn), lambda l, j: (l, 0, j)),
                  pl.BlockSpec((None, 1, tn), lambda l, j: (l, 0, j))],
        out_specs=pl.BlockSpec((None, 8, tn), lambda l, j: (l, 0, j)),
        out_shape=jax.ShapeDtypeStruct((DEPTH, 8, 6 * D_MODEL), F32),
        compiler_params=_params("arbitrary", "arbitrary"),
        name="ada_modulation",
    )(cvec, ada_w, ada_b.reshape(DEPTH, 1, 6 * D_MODEL))


def _norm_mod_kernel(x_ref, g_ref, sc_ref, sh_ref, o_ref):
    y = _rms(x_ref[...]) * g_ref[...]
    o_ref[...] = (y * (1.0 + sc_ref[...]) + sh_ref[...]).astype(o_ref.dtype)


def _mod_spec(l, which, tm):
    return pl.BlockSpec((None, None, None, 1, D_MODEL),
                        lambda i: (l, _group_of_tile(i, tm), which, 0, 0))


def _norm_spec(l, which):
    return pl.BlockSpec((None, None, 1, D_MODEL), lambda i: (l, which, 0, 0))


def norm_mod(x, norms, mod, l, norm_idx, sc_idx, sh_idx):
    tm = 256
    row = pl.BlockSpec((tm, D_MODEL), lambda i: (i, 0))
    return pl.pallas_call(
        _norm_mod_kernel,
        grid=(N_TOK // tm,),
        in_specs=[row, _norm_spec(l, norm_idx), _mod_spec(l, sc_idx, tm), _mod_spec(l, sh_idx, tm)],
        out_specs=row,
        out_shape=jax.ShapeDtypeStruct((N_TOK, D_MODEL), BF16),
        compiler_params=_params("arbitrary"),
        name="norm_mod",
    )(x, norms, mod, mod)


def _resid_norm_kernel(x_ref, y_ref, gy_ref, gate_ref, g2_ref, sc_ref, sh_ref, x1_ref, xn_ref):
    x1 = x_ref[...] + gate_ref[...] * (_rms(y_ref[...]) * gy_ref[...])
    x1_ref[...] = x1
    xn = _rms(x1) * g2_ref[...]
    xn_ref[...] = (xn * (1.0 + sc_ref[...]) + sh_ref[...]).astype(xn_ref.dtype)


def resid_norm(x, y, norms, mod, l, ynorm_idx, gate_idx, l2, norm2_idx, sc_idx, sh_idx):
    tm = 256
    row = pl.BlockSpec((tm, D_MODEL), lambda i: (i, 0))
    return pl.pallas_call(
        _resid_norm_kernel,
        grid=(N_TOK // tm,),
        in_specs=[row, row, _norm_spec(l, ynorm_idx), _mod_spec(l, gate_idx, tm),
                  _norm_spec(l2, norm2_idx), _mod_spec(l2, sc_idx, tm), _mod_spec(l2, sh_idx, tm)],
        out_specs=[row, row],
        out_shape=[jax.ShapeDtypeStruct((N_TOK, D_MODEL), F32),
                   jax.ShapeDtypeStruct((N_TOK, D_MODEL), F32)],
        compiler_params=_params("arbitrary"),
        name="resid_norm",
    )(x, y, norms, mod, norms, mod, mod)


def _mm_kernel(a_ref, w_ref, o_ref):
    o_ref[...] = _dot(a_ref[...], w_ref[...]).astype(o_ref.dtype)


def matmul_cols(a, w, lead, col_off, n_cols, out_dtype, name):
    m, k = a.shape
    tm, tn = 1024, 512
    assert m % tm == 0 and n_cols % tn == 0 and col_off % tn == 0
    nlead = len(lead)
    w_spec = pl.BlockSpec((None,) * nlead + (k, tn), lambda i, j: lead + (0, col_off // tn + j))
    return pl.pallas_call(
        _mm_kernel,
        grid=(m // tm, n_cols // tn),
        in_specs=[pl.BlockSpec((tm, k), lambda i, j: (i, 0)), w_spec],
        out_specs=pl.BlockSpec((tm, tn), lambda i, j: (i, j)),
        out_shape=jax.ShapeDtypeStruct((m, n_cols), out_dtype),
        compiler_params=_params("arbitrary", "arbitrary"),
        name=name,
    )(a, w)


def _gate_kernel(xn_ref, u_ref, wg_ref, wb_ref, o_ref, acc_ref):
    n = pl.program_id(2)
    val = jax.nn.sigmoid(_dot(xn_ref[...], wg_ref[...])) * _dot(u_ref[...], wb_ref[...])

    @pl.when(n == 0)
    def _():
        acc_ref[...] = val

    @pl.when(n > 0)
    def _():
        acc_ref[...] += val

    @pl.when(n == N_BRANCH - 1)
    def _():
        o_ref[...] = acc_ref[...].astype(o_ref.dtype)


def gated_branches(xn, u, w_rest, w_branch, l):
    tm, tn = 1024, 256
    return pl.pallas_call(
        _gate_kernel,
        grid=(N_TOK // tm, D_MODEL // tn, N_BRANCH),
        in_specs=[pl.BlockSpec((tm, D_MODEL), lambda i, j, n: (i, 0)),
                  pl.BlockSpec((tm, BRANCH_W), lambda i, j, n: (i, n)),
                  pl.BlockSpec((D_MODEL, tn), lambda i, j, n: (0, (LP_W + n * D_MODEL) // tn + j)),
                  pl.BlockSpec((None, None, BRANCH_W, tn), lambda i, j, n: (l, n, 0, j))],
        out_specs=pl.BlockSpec((tm, tn), lambda i, j, n: (i, j)),
        out_shape=jax.ShapeDtypeStruct((N_TOK, D_MODEL), BF16),
        scratch_shapes=[pltpu.VMEM((tm, tn), F32)],
        compiler_params=_params("arbitrary", "arbitrary", "arbitrary"),
        name="gated_branches",
    )(xn, u, w_rest, w_branch)


ROUTER_TM = 512


def _router_kernel(x_ref, w_ref, b_ref, tri_ref, idx_ref, gate_ref, rank_ref, cnt_ref, carry_ref):
    @pl.when(pl.program_id(0) == 0)
    def _():
        carry_ref[...] = jnp.zeros_like(carry_ref)

    logits = jnp.dot(x_ref[...].astype(BF16), w_ref[...], preferred_element_type=F32) + b_ref[...]
    lane = lax.broadcasted_iota(jnp.int32, logits.shape, 1).astype(F32)
    cur = jnp.where(lane < N_EXPERTS, logits, -jnp.inf)
    vals, idxs, hots = [], [], []
    for _ in range(TOP_K):
        m = jnp.max(cur, axis=-1, keepdims=True)
        idx = jnp.min(jnp.where(cur == m, lane, float(LANES)), axis=-1, keepdims=True)
        hot = lane == idx
        cur = jnp.where(hot, -jnp.inf, cur)
        vals.append(m)
        idxs.append(idx)
        hots.append(hot)
    exps = [jnp.exp(v - vals[0]) for v in vals]
    denom = exps[0] + exps[1] + exps[2] + exps[3]
    sel = jnp.zeros(logits.shape, F32)
    for hot in hots:
        sel = sel + hot.astype(F32)
    before = jnp.dot(tri_ref[...], sel.astype(BF16), preferred_element_type=F32) + carry_ref[...]
    idx_out = jnp.zeros(logits.shape, F32)
    gate_out = jnp.zeros(logits.shape, F32)
    rank_out = jnp.zeros(logits.shape, F32)
    for k in range(TOP_K):
        rank = jnp.sum(jnp.where(hots[k], before, 0.0), axis=-1, keepdims=True)
        idx_out = jnp.where(lane == k, idxs[k], idx_out)
        gate_out = jnp.where(lane == k, exps[k] / denom, gate_out)
        rank_out = jnp.where(lane == k, rank, rank_out)
    idx_ref[...] = idx_out.astype(jnp.int32)
    gate_ref[...] = gate_out
    rank_ref[...] = rank_out.astype(jnp.int32)
    carry_ref[...] += jnp.sum(sel, axis=0, keepdims=True)
    cnt_ref[...] = carry_ref[...].astype(jnp.int32)


def moe_route(xn2, router_w, router_b, l):
    tm = ROUTER_TM
    w = jnp.zeros((D_MODEL, LANES), BF16).at[:, :N_EXPERTS].set(router_w[l].astype(BF16))
    b = jnp.zeros((1, LANES), F32).at[0, :N_EXPERTS].set(router_b[l])
    tri = (lax.broadcasted_iota(jnp.int32, (tm, tm), 1)
           < lax.broadcasted_iota(jnp.int32, (tm, tm), 0)).astype(BF16)
    row = pl.BlockSpec((tm, LANES), lambda i: (i, 0))
    idx, gates, rank, cnt = pl.pallas_call(
        _router_kernel,
        grid=(N_TOK // tm,),
        in_specs=[pl.BlockSpec((tm, D_MODEL), lambda i: (i, 0)),
                  pl.BlockSpec((D_MODEL, LANES), lambda i: (0, 0)),
                  pl.BlockSpec((1, LANES), lambda i: (0, 0)),
                  pl.BlockSpec((tm, tm), lambda i: (0, 0))],
        out_specs=[row, row, row, pl.BlockSpec((1, LANES), lambda i: (0, 0))],
        out_shape=[jax.ShapeDtypeStruct((N_TOK, LANES), jnp.int32),
                   jax.ShapeDtypeStruct((N_TOK, LANES), F32),
                   jax.ShapeDtypeStruct((N_TOK, LANES), jnp.int32),
                   jax.ShapeDtypeStruct((1, LANES), jnp.int32)],
        scratch_shapes=[pltpu.VMEM((1, LANES), F32)],
        compiler_params=_params("arbitrary"),
        name="moe_route",
    )(xn2, w, b, tri)
    return idx[:, :TOP_K], gates[:, :TOP_K], rank[:, :TOP_K], cnt[0, :N_EXPERTS]


def _row_copy(src_hbm, dst_vmem, src_row, dst_row, sem):
    return pltpu.make_async_copy(src_hbm.at[pl.ds(src_row, 1), :], dst_vmem.at[pl.ds(dst_row, 1), :], sem)


def _gather_rows_kernel(tok_ref, nused_ref, x_hbm, o_ref, buf, sem):
    i = pl.program_id(0)
    n_used = nused_ref[0]

    def start_block(j):
        def start(r, c):
            _row_copy(x_hbm, buf.at[j % 2], tok_ref[j * MOE_ROWS_BLK + r], r, sem.at[j % 2]).start()
            return c
        lax.fori_loop(0, MOE_ROWS_BLK, start, 0, unroll=8)

    @pl.when(i == 0)
    def _():
        start_block(i)

    @pl.when(i + 1 < n_used)
    def _():
        start_block(i + 1)

    @pl.when(i < n_used)
    def _():
        def wait(r, c):
            _row_copy(x_hbm, buf.at[i % 2], 0, r, sem.at[i % 2]).wait()
            return c
        lax.fori_loop(0, MOE_ROWS_BLK, wait, 0, unroll=8)
        o_ref[...] = buf[i % 2].astype(o_ref.dtype)

    @pl.when(i >= n_used)
    def _():
        o_ref[...] = jnp.zeros_like(o_ref)


def gather_rows(row_token, n_used, x):
    width = x.shape[1]
    return pl.pallas_call(
        _gather_rows_kernel,
        grid_spec=pltpu.PrefetchScalarGridSpec(
            num_scalar_prefetch=2,
            grid=(MOE_BLOCKS,),
            in_specs=[pl.BlockSpec(memory_space=pl.ANY)],
            out_specs=pl.BlockSpec((MOE_ROWS_BLK, width), lambda i, tok, nu: (i, 0)),
            scratch_shapes=[pltpu.VMEM((2, MOE_ROWS_BLK, width), x.dtype), pltpu.SemaphoreType.DMA((2,))]),
        out_shape=jax.ShapeDtypeStruct((MOE_ROWS, width), BF16),
        compiler_params=_params("arbitrary"),
        name="moe_gather_rows",
    )(row_token, n_used, x)


UP_TF = 512
DOWN_TN = 2048
N_ITEM_ARGS = 9


def _stream_expert_weights(q, items, copies, on_ready):
    ie, jt, _, first, _, nxt_e, nxt_t, has_nxt, slot = items

    @pl.when(q == 0)
    def _():
        for cp in copies(ie[0], jt[0], 0):
            cp.start()

    @pl.when(first[q] == 1)
    def _():
        s = slot[q]

        @pl.when(has_nxt[q] == 1)
        def _():
            for cp in copies(nxt_e[q], nxt_t[q], 1 - s):
                cp.start()

        for cp in copies(ie[q], jt[q], s):
            cp.wait()
        on_ready(s)


def _expert_up_kernel(*refs, l):
    items = refs[:N_ITEM_ARGS]
    x_ref, wg_hbm, wu_hbm, bg_ref, bu_ref, h_ref, wg_buf, wu_buf, wg_bf, wu_bf, sem = refs[N_ITEM_ARGS:]
    valid_ref = items[4]
    q = pl.program_id(0)

    def copies(e, tile, s):
        cols = pl.ds(tile * UP_TF, UP_TF)
        return (pltpu.make_async_copy(wg_hbm.at[l, e, :, cols], wg_buf.at[s], sem.at[0, s]),
                pltpu.make_async_copy(wu_hbm.at[l, e, :, cols], wu_buf.at[s], sem.at[1, s]))

    def to_bf16(s):
        wg_bf[...] = wg_buf[s].astype(BF16)
        wu_bf[...] = wu_buf[s].astype(BF16)

    _stream_expert_weights(q, items, copies, to_bf16)

    @pl.when(valid_ref[q] == 1)
    def _():
        x = x_ref[...]
        gate = jnp.dot(x, wg_bf[...], preferred_element_type=F32) + bg_ref[...]
        up = jnp.dot(x, wu_bf[...], preferred_element_type=F32) + bu_ref[...]
        gate = jnp.minimum(gate, SWIGLU_LIMIT)
        up = jnp.clip(up, -SWIGLU_LIMIT, SWIGLU_LIMIT)
        glu = gate * jax.nn.sigmoid(gate * SWIGLU_ALPHA)
        h_ref[...] = ((up + 1.0) * glu).astype(h_ref.dtype)

    @pl.when(valid_ref[q] == 0)
    def _():
        h_ref[...] = jnp.zeros_like(h_ref)


def expert_up(items, xs, w_gate, b_gate, w_up, b_up, l):
    tf = UP_TF
    n_items = items[0].shape[0]
    w_spec = pl.BlockSpec(memory_space=pl.ANY)
    b_spec = pl.BlockSpec((None, None, 1, tf), lambda q, *it: (l, it[0][q], 0, it[1][q]))
    return pl.pallas_call(
        functools.partial(_expert_up_kernel, l=l),
        grid_spec=pltpu.PrefetchScalarGridSpec(
            num_scalar_prefetch=N_ITEM_ARGS,
            grid=(n_items,),
            in_specs=[pl.BlockSpec((MOE_ROWS_BLK, D_MODEL), lambda q, *it: (it[2][q], 0)),
                      w_spec, w_spec, b_spec, b_spec],
            out_specs=pl.BlockSpec((MOE_ROWS_BLK, tf), lambda q, *it: (it[2][q], it[1][q])),
            scratch_shapes=[pltpu.VMEM((2, D_MODEL, tf), F32), pltpu.VMEM((2, D_MODEL, tf), F32),
                            pltpu.VMEM((D_MODEL, tf), BF16), pltpu.VMEM((D_MODEL, tf), BF16),
                            pltpu.SemaphoreType.DMA((2, 2))]),
        out_shape=jax.ShapeDtypeStruct((MOE_ROWS, EXPERT_FF), BF16),
        compiler_params=_params("arbitrary"),
        name="moe_expert_up",
    )(*items, xs, w_gate, w_up,
      b_gate.reshape(DEPTH, N_EXPERTS, 1, EXPERT_FF), b_up.reshape(DEPTH, N_EXPERTS, 1, EXPERT_FF))


def _expert_down_kernel(*refs, l):
    items = refs[:N_ITEM_ARGS]
    h_ref, wd_hbm, bd_ref, y_ref, wd_buf, wd_bf, sem = refs[N_ITEM_ARGS:]
    valid_ref = items[4]
    q = pl.program_id(0)

    def copies(e, tile, s):
        return (pltpu.make_async_copy(wd_hbm.at[l, e, :, pl.ds(tile * DOWN_TN, DOWN_TN)], wd_buf.at[s],
                                      sem.at[s]),)

    def to_bf16(s):
        wd_bf[...] = wd_buf[s].astype(BF16)

    _stream_expert_weights(q, items, copies, to_bf16)

    @pl.when(valid_ref[q] == 1)
    def _():
        y_ref[...] = jnp.dot(h_ref[...], wd_bf[...], preferred_element_type=F32) + bd_ref[...]

    @pl.when(valid_ref[q] == 0)
    def _():
        y_ref[...] = jnp.zeros_like(y_ref)


def expert_down(items, h, w_down, b_down, l):
    tn = DOWN_TN
    n_items = items[0].shape[0]
    return pl.pallas_call(
        functools.partial(_expert_down_kernel, l=l),
        grid_spec=pltpu.PrefetchScalarGridSpec(
            num_scalar_prefetch=N_ITEM_ARGS,
            grid=(n_items,),
            in_specs=[pl.BlockSpec((MOE_ROWS_BLK, EXPERT_FF), lambda q, *it: (it[2][q], 0)),
                      pl.BlockSpec(memory_space=pl.ANY),
                      pl.BlockSpec((None, None, 1, tn), lambda q, *it: (l, it[0][q], 0, it[1][q]))],
            out_specs=pl.BlockSpec((MOE_ROWS_BLK, tn), lambda q, *it: (it[2][q], it[1][q])),
            scratch_shapes=[pltpu.VMEM((2, EXPERT_FF, tn), F32), pltpu.VMEM((EXPERT_FF, tn), BF16),
                            pltpu.SemaphoreType.DMA((2,))]),
        out_shape=jax.ShapeDtypeStruct((MOE_ROWS, D_MODEL), F32),
        compiler_params=_params("arbitrary"),
        name="moe_expert_down",
    )(*items, h, w_down, b_down.reshape(DEPTH, N_EXPERTS, 1, D_MODEL))


COMBINE_TM = 128


def _combine_kernel(pos_ref, y_hbm, g_ref, x_ref, gy_ref, gate_ref, g2_ref, sc_ref, sh_ref,
                    x1_ref, xn_ref, buf, sem):
    i = pl.program_id(0)

    def start_tile(j):
        def start(r, c):
            for k in range(TOP_K):
                _row_copy(y_hbm, buf.at[j % 2].at[k], pos_ref[(j * COMBINE_TM + r) * TOP_K + k], r,
                          sem.at[j % 2]).start()
            return c
        lax.fori_loop(0, COMBINE_TM, start, 0, unroll=4)

    @pl.when(i == 0)
    def _():
        start_tile(i)

    @pl.when(i + 1 < pl.num_programs(0))
    def _():
        start_tile(i + 1)

    def wait(r, c):
        for k in range(TOP_K):
            _row_copy(y_hbm, buf.at[i % 2].at[k], 0, r, sem.at[i % 2]).wait()
        return c

    lax.fori_loop(0, COMBINE_TM, wait, 0, unroll=4)
    g = g_ref[...]
    ff = buf[i % 2, 0] * g[:, 0:1]
    for k in range(1, TOP_K):
        ff = ff + buf[i % 2, k] * g[:, k:k + 1]
    x1 = x_ref[...] + gate_ref[...] * (_rms(ff) * gy_ref[...])
    x1_ref[...] = x1
    xn = _rms(x1) * g2_ref[...]
    xn_ref[...] = (xn * (1.0 + sc_ref[...]) + sh_ref[...]).astype(xn_ref.dtype)


def moe_combine(pos_flat, yb, gates, x, norms, mod, l, l2):
    tm = COMBINE_TM
    row = pl.BlockSpec((tm, D_MODEL), lambda i, p: (i, 0))

    def mod_spec(ll, which):
        return pl.BlockSpec((None, None, None, 1, D_MODEL),
                            lambda i, p: (ll, _group_of_tile(i, tm), which, 0, 0))

    def norm_spec(ll, which):
        return pl.BlockSpec((None, None, 1, D_MODEL), lambda i, p: (ll, which, 0, 0))

    return pl.pallas_call(
        _combine_kernel,
        grid_spec=pltpu.PrefetchScalarGridSpec(
            num_scalar_prefetch=1,
            grid=(N_TOK // tm,),
            in_specs=[pl.BlockSpec(memory_space=pl.ANY),
                      pl.BlockSpec((tm, TOP_K), lambda i, p: (i, 0)),
                      row, norm_spec(l, 3), mod_spec(l, 5), norm_spec(l2, 0), mod_spec(l2, 1), mod_spec(l2, 0)],
            out_specs=[row, row],
            scratch_shapes=[pltpu.VMEM((2, TOP_K, tm, D_MODEL), F32), pltpu.SemaphoreType.DMA((2,))]),
        out_shape=[jax.ShapeDtypeStruct((N_TOK, D_MODEL), F32),
                   jax.ShapeDtypeStruct((N_TOK, D_MODEL), BF16)],
        compiler_params=_params("arbitrary"),
        name="moe_combine",
    )(pos_flat, yb, gates, x, norms, mod, norms, mod, mod)


def _work_items(n_blk_e, n_tiles):
    total = MOE_BLOCKS * n_tiles
    blk_end = jnp.cumsum(n_blk_e)
    blk_start = blk_end - n_blk_e
    n_used = blk_end[-1]
    used = n_used * n_tiles
    q = jnp.arange(total, dtype=jnp.int32)
    valid = q < used
    qc = jnp.minimum(q, used - 1)
    e = jnp.sum((qc[:, None] >= (blk_end * n_tiles)[None, :]).astype(jnp.int32), axis=1)
    rel = qc - blk_start[e] * n_tiles
    nb = jnp.maximum(n_blk_e[e], 1)
    n_tail = jnp.maximum(MOE_BLOCKS - n_used, 1)
    rel_tail = q - used
    tile = jnp.where(valid, rel // nb, rel_tail // n_tail)
    blk = jnp.where(valid, blk_start[e] + rel % nb, n_used + rel_tail % n_tail)
    first = (rel % nb == 0) & valid
    tile = tile.astype(jnp.int32)
    first_pos = jnp.where(first, q, total)
    nxt = lax.cummin(jnp.concatenate([first_pos[1:], jnp.full((1,), total, jnp.int32)]), reverse=True)
    has_nxt = nxt < total
    nxt = jnp.minimum(nxt, total - 1)
    slot = (jnp.cumsum(first.astype(jnp.int32)) - 1) % 2
    as_i32 = lambda t: t.astype(jnp.int32)
    return (e, tile, as_i32(blk), as_i32(first), as_i32(valid), e[nxt], tile[nxt], as_i32(has_nxt),
            as_i32(slot))


def moe_ffn(xn2, x1, norms, mod, l, l2, router_w, router_b, w_gate, b_gate, w_up, b_up, w_down, b_down):
    idx, gates, rank, counts = moe_route(xn2, router_w, router_b, l)
    n_blk_e = (counts + MOE_ROWS_BLK - 1) // MOE_ROWS_BLK
    pstart = (jnp.cumsum(n_blk_e) - n_blk_e) * MOE_ROWS_BLK
    pos = pstart[idx] + rank
    tok = jnp.broadcast_to(jnp.arange(N_TOK, dtype=jnp.int32)[:, None], pos.shape)
    row_token = jnp.zeros((MOE_ROWS,), jnp.int32).at[pos.reshape(-1)].set(tok.reshape(-1))
    xs = gather_rows(row_token, jnp.sum(n_blk_e).reshape(1).astype(jnp.int32), xn2)
    h = expert_up(_work_items(n_blk_e, EXPERT_FF // UP_TF), xs, w_gate, b_gate, w_up, b_up, l)
    yb = expert_down(_work_items(n_blk_e, D_MODEL // DOWN_TN), h, w_down, b_down, l)
    return moe_combine(pos.reshape(-1).astype(jnp.int32), yb, gates, x1, norms, mod, l, l2)


HALF = RWKV_HEAD
WKV_L = 64
SOLVE_BLK = 16
N_PAIR = BRANCH_W // LANES
MIX_TM = 256


def _head_sum(x):
    li = lax.broadcasted_iota(jnp.int32, (LANES, LANES), 0) // HALF
    lj = lax.broadcasted_iota(jnp.int32, (LANES, LANES), 1) // HALF
    ones_bd = (li == lj).astype(BF16)
    hi, lo = _split_bf16(x)
    parts = []
    for p in range(x.shape[1] // LANES):
        sl = slice(p * LANES, (p + 1) * LANES)
        parts.append(jnp.dot(hi[:, sl], ones_bd, preferred_element_type=F32)
                     + jnp.dot(lo[:, sl], ones_bd, preferred_element_type=F32))
    return jnp.concatenate(parts, axis=1)


def _rwkv_prep_kernel(z_ref, zp_ref, zn_ref, vf_ref, mu_ref, w0_ref, w2_ref, a0_ref, a2_ref, kk_ref, ka_ref,
                      rk_ref, g2_ref, v0_ref, v1_ref, v2_ref,
                      r_o, v_o, kk_o, g_o, bonus_o, lwf_o, kf_o, bf_o, lwb_o, kb_o, bb_o,
                      *, n_ctx_blocks, lat_blocks, has_vres):
    i = pl.program_id(0)
    j = jnp.maximum(i - n_ctx_blocks, 0) % lat_blocks
    is_lat = i >= n_ctx_blocks
    has_prev = jnp.logical_and(is_lat, j > 0).astype(F32)
    has_next = jnp.logical_and(is_lat, j < lat_blocks - 1).astype(F32)
    z = z_ref[...]
    tm = z.shape[0]
    row = lax.broadcasted_iota(jnp.int32, z.shape, 0)
    prev_row = zp_ref[7:8, :] * has_prev
    next_row = zn_ref[0:1, :] * has_next
    z_up = jnp.where(row == 0, prev_row, pltpu.roll(z, 1, 0))
    z_dn = jnp.where(row == tm - 1, next_row, pltpu.roll(z, tm - 1, 0))
    z = z + mu_ref[...] * (0.5 * (z_up + z_dn) - z)

    C = BRANCH_W
    r, k, v = z[:, 0:C], z[:, C:2 * C], z[:, 2 * C:3 * C]
    lora = z[:, 3 * C:3 * C + 2 * LANES]
    gd = z[:, 3 * C + 2 * LANES:3 * C + 4 * LANES]
    if has_vres:
        mixv = jax.nn.sigmoid(v0_ref[...] + _dot(_dot(v, v1_ref[...]), v2_ref[...]))
        v = v + (vf_ref[...] - v) * mixv
    g = _dot(jax.nn.sigmoid(gd), g2_ref[...])
    kk = k * kk_ref[...]
    kk = kk / jnp.maximum(jnp.sqrt(_head_sum(kk * kk)), 1e-12)
    r_o[...] = r
    v_o[...] = v
    kk_o[...] = kk
    g_o[...] = g
    bonus_o[...] = _head_sum(r * k * rk_ref[...]) * v
    tw = jnp.tanh(lora[:, :LANES])
    ad = lora[:, LANES:]
    for d, (lw_o, k_o, b_o) in enumerate(((lwf_o, kf_o, bf_o), (lwb_o, kb_o, bb_o))):
        w_log = -_softplus(-(w0_ref[d:d + 1, :] + _dot(tw, w2_ref[d]))) - 0.5
        lw_o[...] = -jnp.exp(w_log)
        rate = jax.nn.sigmoid(a0_ref[d:d + 1, :] + _dot(ad, a2_ref[d]))
        k_o[...] = k * (1.0 + (rate - 1.0) * ka_ref[d:d + 1, :])
        b_o[...] = kk * rate


def rwkv_prep(z_r, v_first, l, n_ctx_blocks, lat_blocks, rwkv_mu, rwkv_w0, rwkv_w2, rwkv_a0, rwkv_a2,
              rwkv_kk, rwkv_ka, rwkv_rk, rwkv_g2, rwkv_v0, rwkv_v1, rwkv_v2):
    n = z_r.shape[0]
    tm = MIX_TM
    has_vres = l > 0
    C = BRANCH_W
    mu = jnp.zeros((1, RWKV_PAD), F32).at[0, :RWKV_IN].set(rwkv_mu[l])
    w2 = jnp.zeros((2, LANES, C), F32)
    a2 = jnp.zeros((2, LANES, C), F32)
    for d in range(2):
        w2 = w2.at[d, d * RWKV_W_RANK:(d + 1) * RWKV_W_RANK].set(rwkv_w2[l, d])
        a2 = a2.at[d, d * RWKV_A_RANK:(d + 1) * RWKV_A_RANK].set(rwkv_a2[l, d])
    g2 = jnp.zeros((2 * LANES, C), F32).at[:RWKV_G_RANK].set(rwkv_g2[l])
    lv = max(l - 1, 0)
    v0 = rwkv_v0[lv].reshape(1, C)
    v1 = jnp.zeros((C, LANES), F32).at[:, :RWKV_V_RANK].set(rwkv_v1[lv])
    v2 = jnp.zeros((LANES, C), F32).at[:RWKV_V_RANK].set(rwkv_v2[lv])
    if v_first is None:
        v_first = jnp.zeros((8, C), F32)
        vf_spec = pl.BlockSpec((8, C), lambda i: (0, 0))
    else:
        vf_spec = pl.BlockSpec((tm, C), lambda i: (i, 0))
    halo = tm // 8
    last8 = n // 8 - 1

    def full(shape):
        return pl.BlockSpec(shape, lambda i: (0,) * len(shape))

    out_row = pl.BlockSpec((tm, C), lambda i: (i, 0))
    return pl.pallas_call(
        functools.partial(_rwkv_prep_kernel, n_ctx_blocks=n_ctx_blocks, lat_blocks=lat_blocks,
                          has_vres=has_vres),
        grid=(n // tm,),
        in_specs=[pl.BlockSpec((tm, RWKV_PAD), lambda i: (i, 0)),
                  pl.BlockSpec((8, RWKV_PAD), lambda i: (jnp.maximum(i * halo - 1, 0), 0)),
                  pl.BlockSpec((8, RWKV_PAD), lambda i: (jnp.minimum((i + 1) * halo, last8), 0)),
                  vf_spec, full((1, RWKV_PAD)), full((2, C)), full((2, LANES, C)), full((2, C)),
                  full((2, LANES, C)), full((1, C)), full((2, C)), full((1, C)), full((2 * LANES, C)),
                  full((1, C)), full((C, LANES)), full((LANES, C))],
        out_specs=[out_row] * 11,
        out_shape=[jax.ShapeDtypeStruct((n, C), F32)] * 11,
        compiler_params=_params("arbitrary"),
        name="rwkv_prep",
    )(z_r, z_r, z_r, v_first, mu, rwkv_w0[l], w2, rwkv_a0[l], a2, rwkv_kk[l].reshape(1, C), rwkv_ka[l],
      rwkv_rk[l].reshape(1, C), g2, v0, v1, v2)


def _bd(x):
    lane = lax.broadcasted_iota(jnp.int32, x.shape, 1)
    return jnp.concatenate([jnp.where(lane < HALF, x, 0.0), jnp.where(lane >= HALF, x, 0.0)], axis=0)


def _dot_hp(a, b):
    a_hi, a_lo = _split_bf16(a)
    b_hi, b_lo = _split_bf16(b)
    bd_hi = _bd(b_hi)
    return (jnp.dot(a_hi, bd_hi, preferred_element_type=F32)
            + jnp.dot(a_lo, bd_hi, preferred_element_type=F32)
            + jnp.dot(a_hi, _bd(b_lo), preferred_element_type=F32))


def _wkv_direction(r, v, kk, lw, k, b, s_ref, d, rev, y_ref):
    L = WKV_L
    ti = lax.broadcasted_iota(jnp.int32, (L, L), 0)
    si = lax.broadcasted_iota(jnp.int32, (L, L), 1)
    tri = ((si >= ti) if rev else (si <= ti)).astype(BF16)
    cs = _dot_left_exact(tri, lw)
    ctot = cs[0:1] if rev else cs[L - 1:L]
    e_neg = jnp.exp(-cs)
    rt = r * jnp.exp(cs)
    at = -kk * jnp.exp(cs - lw)
    bt = b * e_neg
    kt = k * e_neg
    e_rem = jnp.exp(ctot - cs)
    bh = b * e_rem
    kh = k * e_rem
    wtot = jnp.exp(ctot)
    row = lax.broadcasted_iota(jnp.int32, (L, LANES), 0)
    sidx = lax.broadcasted_iota(jnp.int32, (L, LANES), 1) % HALF
    strict = (sidx > row) if rev else (sidx < row)
    incl = (sidx >= row) if rev else (sidx <= row)
    eye = (sidx == row).astype(F32)
    same_head = (lax.broadcasted_iota(jnp.int32, (LANES, LANES), 0) // HALF
                 == lax.broadcasted_iota(jnp.int32, (LANES, LANES), 1) // HALF)
    pairs = range(N_PAIR)
    sls = [slice(p * LANES, (p + 1) * LANES) for p in pairs]
    S = [s_ref[d, p] for p in pairs]
    ar = [jnp.concatenate([at[:, sl], rt[:, sl]], axis=0) for sl in sls]
    g_b = [_dot_nt(ar[p], _bd(bt[:, sls[p]])) for p in pairs]
    g_k = [_dot_nt(ar[p], _bd(kt[:, sls[p]])) for p in pairs]
    g_s = [_dot_nt(ar[p], S[p]) for p in pairs]
    yield
    n_ab = [jnp.where(strict, g_b[p][:L], 0.0) for p in pairs]
    same_blk = (sidx // SOLVE_BLK) == (row // SOLVE_BLK)
    pw = [jnp.where(same_blk, n_ab[p], 0.0) for p in pairs]
    dinv = [eye + pw[p] for p in pairs]
    step = 2
    while step < SOLVE_BLK:
        pw = [_dot_hp(pw[p], pw[p]) for p in pairs]
        dinv = [dinv[p] + _dot_hp(dinv[p], pw[p]) for p in pairs]
        step *= 2
        yield
    vb = [_bd(v[:, sl]) for sl in sls]
    l_ak = [jnp.where(strict, g_k[p][:L], 0.0) for p in pairs]
    rhs = [g_s[p][:L] + _dot(l_ak[p], vb[p]) for p in pairs]
    yield
    n_blk = L // SOLVE_BLK
    zeros_blk = jnp.zeros((SOLVE_BLK, LANES), F32)
    done = [[None] * n_blk for _ in pairs]

    def rows_with(blocks):
        return jnp.concatenate([zeros_blk if b is None else b for b in blocks], axis=0)

    for kb in (range(n_blk - 1, -1, -1) if rev else range(n_blk)):
        rows = slice(kb * SOLVE_BLK, (kb + 1) * SOLVE_BLK)
        for p in pairs:
            acc = rhs[p][rows]
            if any(b is not None for b in done[p]):
                acc = acc + _dot(n_ab[p][rows], _bd(rows_with(done[p])))
            only = [None] * n_blk
            only[kb] = acc
            done[p][kb] = _dot(dinv[p][rows], _bd(rows_with(only)))
        yield
    ut = [rows_with(done[p]) for p in pairs]
    for p in pairs:
        l_rb = jnp.where(incl, g_b[p][L:], 0.0)
        l_rk = jnp.where(incl, g_k[p][L:], 0.0)
        y_ref[:, sls[p]] = g_s[p][L:] + _dot(l_rb, _bd(ut[p])) + _dot(l_rk, vb[p])
    yield
    for p in pairs:
        upd = _dot(jnp.concatenate([ut[p], v[:, sls[p]]], axis=0).T,
                   jnp.concatenate([bh[:, sls[p]], kh[:, sls[p]]], axis=0))
        s_ref[d, p] = S[p] * wtot[:, sls[p]] + jnp.where(same_head, upd, 0.0)
    yield


def _wkv_kernel(rf_ref, vf_ref, kkf_ref, lwf_ref, kf_ref, bf_ref, rb_ref, vb_ref, kkb_ref, lwb_ref, kb_ref,
                bb_ref, s0_ref, yf_ref, yb_ref, st_ref, s_scr):
    c = pl.program_id(1)

    @pl.when(c == 0)
    def _():
        s_scr[...] = s0_ref[...]

    fwd = _wkv_direction(rf_ref[...], vf_ref[...], kkf_ref[...], lwf_ref[...], kf_ref[...], bf_ref[...],
                         s_scr, 0, False, yf_ref)
    bwd = _wkv_direction(rb_ref[...], vb_ref[...], kkb_ref[...], lwb_ref[...], kb_ref[...], bb_ref[...],
                         s_scr, 1, True, yb_ref)
    for _ in zip(fwd, bwd):
        pass

    @pl.when(c == pl.num_programs(1) - 1)
    def _():
        st_ref[...] = s_scr[...]


def wkv_scan(prep, s0_bd, row_off, n_seq, t_len):
    r, v, kk, lwf, kf, bf, lwb, kb, bb = prep
    L = WKV_L
    nc = t_len // L
    base = row_off // L
    rows = n_seq * t_len
    fwd = pl.BlockSpec((L, BRANCH_W), lambda s, c: (base + s * nc + c, 0))
    bwd = pl.BlockSpec((L, BRANCH_W), lambda s, c: (base + s * nc + (nc - 1 - c), 0))
    y_fwd = pl.BlockSpec((L, BRANCH_W), lambda s, c: (s * nc + c, 0))
    y_bwd = pl.BlockSpec((L, BRANCH_W), lambda s, c: (s * nc + (nc - 1 - c), 0))
    st = pl.BlockSpec((None, 2, N_PAIR, LANES, LANES), lambda s, c: (s, 0, 0, 0, 0))
    return pl.pallas_call(
        _wkv_kernel,
        grid=(n_seq, nc),
        in_specs=[fwd] * 6 + [bwd] * 6 + [st],
        out_specs=[y_fwd, y_bwd, st],
        out_shape=[jax.ShapeDtypeStruct((rows, BRANCH_W), F32), jax.ShapeDtypeStruct((rows, BRANCH_W), F32),
                   jax.ShapeDtypeStruct((n_seq, 2, N_PAIR, LANES, LANES), F32)],
        scratch_shapes=[pltpu.VMEM((2, N_PAIR, LANES, LANES), F32)],
        compiler_params=_params("arbitrary", "arbitrary"),
        name="wkv_scan",
    )(r, v, kk, lwf, kf, bf, r, v, kk, lwb, kb, bb, s0_bd)


def pair_states(S):
    out = jnp.zeros((S.shape[0], 2, N_PAIR, LANES, LANES), F32)
    out = out.at[:, :, :, :HALF, :HALF].set(S[:, :, 0::2])
    return out.at[:, :, :, HALF:, HALF:].set(S[:, :, 1::2])


def unpair_states(Sp):
    both = jnp.stack([Sp[:, :, :, :HALF, :HALF], Sp[:, :, :, HALF:, HALF:]], axis=3)
    return both.reshape(Sp.shape[0], 2, RWKV_H, RWKV_HEAD, RWKV_HEAD)


def _rwkv_post_kernel(yf_ref, yb_ref, bonus_ref, g_ref, gw_ref, gb_ref, o_ref):
    y = yf_ref[...] + yb_ref[...]
    inv_n = 1.0 / RWKV_HEAD
    d = y - _head_sum(y) * inv_n
    var = _head_sum(d * d) * inv_n
    yn = d * lax.rsqrt(var + RWKV_GN_EPS) * gw_ref[...] + gb_ref[...]
    o_ref[...] = ((yn + bonus_ref[...]) * g_ref[...]).astype(o_ref.dtype)


def rwkv_post(yf, yb, bonus, g, gn_w, gn_b):
    n = yf.shape[0]
    tm = MIX_TM
    C = BRANCH_W
    row = pl.BlockSpec((tm, C), lambda i: (i, 0))
    vec = pl.BlockSpec((1, C), lambda i: (0, 0))
    return pl.pallas_call(
        _rwkv_post_kernel,
        grid=(n // tm,),
        in_specs=[row, row, row, row, vec, vec],
        out_specs=row,
        out_shape=jax.ShapeDtypeStruct((n, C), BF16),
        compiler_params=_params("arbitrary"),
        name="rwkv_post",
    )(yf, yb, bonus, g, gn_w.reshape(1, C), gn_b.reshape(1, C))


def _expm1(x):
    small = x * (1.0 + x * (0.5 + x * (1.0 / 6.0 + x * (1.0 / 24.0 + x * (1.0 / 120.0)))))
    return jnp.where(jnp.abs(x) < 0.1, small, jnp.exp(x) - 1.0)


def _gelu_tanh(x):
    return 0.5 * x * (1.0 + jnp.tanh(0.7978845608028654 * (x + 0.044715 * x * x * x)))


def _lru_kernel(zx_ref, zg_ref, h0_ref, cw_ref, cb_ref, wa_ref, ba_ref, wx_ref, bx_ref, lam_ref,
                y_ref, hT_ref, af, hf, ab, hb):
    T = zx_ref.shape[0]
    x = zx_ref[...]
    row = lax.broadcasted_iota(jnp.int32, x.shape, 0)
    xc = cb_ref[...] + x * cw_ref[1:2, :]
    xc = xc + jnp.where(row >= 1, pltpu.roll(x, 1, 0), 0.0) * cw_ref[0:1, :]
    xc = xc + jnp.where(row < T - 1, pltpu.roll(x, T - 1, 0), 0.0) * cw_ref[2:3, :]
    xc = xc + jnp.where(row < T - 2, pltpu.roll(x, T - 2, 0), 0.0) * cw_ref[3:4, :]
    for d, (a_s, h_s) in enumerate(((af, hf), (ab, hb))):
        rg = jax.nn.sigmoid(_dot(xc, wa_ref[d]) + ba_ref[d:d + 1, :])
        ig = jax.nn.sigmoid(_dot(xc, wx_ref[d]) + bx_ref[d:d + 1, :])
        log_a = -LRU_C * rg * _softplus(-lam_ref[d:d + 1, :])
        a_s[...] = jnp.exp(log_a)
        h_s[...] = jnp.sqrt(-_expm1(2.0 * log_a)) * (ig * xc)

    def step(t, carry):
        h_f, h_b = carry
        tb = T - 1 - t
        h_f = af[pl.ds(t, 1), :] * h_f + hf[pl.ds(t, 1), :]
        h_b = ab[pl.ds(tb, 1), :] * h_b + hb[pl.ds(tb, 1), :]
        hf[pl.ds(t, 1), :] = h_f
        hb[pl.ds(tb, 1), :] = h_b
        return h_f, h_b

    h_f, h_b = lax.fori_loop(0, T, step, (h0_ref[0:1, :], h0_ref[1:2, :]), unroll=8)
    hT_ref[0:1, :] = h_f
    hT_ref[1:2, :] = h_b
    y_ref[...] = ((hf[...] + hb[...]) * _gelu_tanh(zg_ref[...])).astype(y_ref.dtype)


def rglru_mixer(z_lp, h0, row_off, n_seq, t_len, l, lru_conv_w, lru_conv_b, lru_wa, lru_ba, lru_wx, lru_bx,
                lru_lam):
    W, BW = LRU_W, LRU_BW
    base = row_off // t_len
    st = pl.BlockSpec((None, 2, BW), lambda s, b: (s, 0, b))
    vec2 = pl.BlockSpec((None, 2, BW), lambda s, b: (l, 0, b))
    mat = pl.BlockSpec((None, 2, None, BW, BW), lambda s, b: (l, 0, b, 0, 0))
    return pl.pallas_call(
        _lru_kernel,
        grid=(n_seq, LRU_BLOCKS),
        in_specs=[pl.BlockSpec((t_len, BW), lambda s, b: (base + s, b)),
                  pl.BlockSpec((t_len, BW), lambda s, b: (base + s, LRU_BLOCKS + b)),
                  st,
                  pl.BlockSpec((None, LRU_CONV, BW), lambda s, b: (l, 0, b)),
                  pl.BlockSpec((None, 1, BW), lambda s, b: (l, 0, b)),
                  mat, vec2, mat, vec2, vec2],
        out_specs=[pl.BlockSpec((t_len, BW), lambda s, b: (s, b)), st],
        out_shape=[jax.ShapeDtypeStruct((n_seq * t_len, W), BF16), jax.ShapeDtypeStruct((n_seq, 2, W), F32)],
        scratch_shapes=[pltpu.VMEM((t_len, BW), F32)] * 4,
        compiler_params=_params("arbitrary", "arbitrary"),
        name="rglru_mixer",
    )(z_lp, z_lp, h0, lru_conv_w, lru_conv_b.reshape(DEPTH, 1, W), lru_wa, lru_ba, lru_wx, lru_bx, lru_lam)


def _fnet_kernel(z_ref, ct_ref, st_ref, cc_ref, sc_ref, w_ref, o_ref):
    z = z_ref[...]
    T = z.shape[0]
    zc = _dot(z, cc_ref[...])
    zs = _dot(z, sc_ref[...])
    f = (_dot(ct_ref[...], zc) - _dot(st_ref[...], zs)) * (1.0 / (T * FNET_GW) ** 0.5)
    o_ref[...] = _dot(f, w_ref[...]).astype(o_ref.dtype)


def _dft(n):
    idx = jnp.arange(n, dtype=jnp.int32)
    ang = ((idx[:, None] * idx[None, :]) % n).astype(F32) * (2.0 * jnp.pi / n)
    return jnp.cos(ang).astype(BF16), jnp.sin(ang).astype(BF16)


def fourier_mixer(z_f, row_off, n_seq, t_len, l, fnet_w):
    base = row_off // t_len
    ct, st = _dft(t_len)
    cc, sc = _dft(FNET_GW)
    sq = lambda n: pl.BlockSpec((n, n), lambda s, g: (0, 0))
    return pl.pallas_call(
        _fnet_kernel,
        grid=(n_seq, FNET_GROUPS),
        in_specs=[pl.BlockSpec((t_len, FNET_GW), lambda s, g: (base + s, g)),
                  sq(t_len), sq(t_len), sq(FNET_GW), sq(FNET_GW),
                  pl.BlockSpec((None, None, FNET_GW, FNET_GW), lambda s, g: (l, g, 0, 0))],
        out_specs=pl.BlockSpec((t_len, FNET_GW), lambda s, g: (s, g)),
        out_shape=jax.ShapeDtypeStruct((n_seq * t_len, BRANCH_W), BF16),
        compiler_params=_params("arbitrary", "arbitrary"),
        name="fourier_mixer",
    )(z_f, ct, st, cc, sc, fnet_w)


def _pool_kernel(z_ref, w_ref, sc_ref, o_ref):
    z = z_ref[...]
    T = z.shape[0]
    half = jnp.left_shift(1, pl.program_id(1))
    ti = lax.broadcasted_iota(jnp.int32, (T, T), 0)
    si = lax.broadcasted_iota(jnp.int32, (T, T), 1)
    band = jnp.logical_and(si >= ti - half, si < ti + half).astype(BF16)
    t1 = lax.broadcasted_iota(jnp.int32, (T, 1), 0)
    cnt = (jnp.minimum(t1 + half, T) - jnp.maximum(t1 - half, 0)).astype(F32)
    d = _dot_left_exact(band, z) / cnt - z
    o_ref[...] = (_dot(d, w_ref[...]) * sc_ref[...]).astype(o_ref.dtype)


def pool_mixer(z_lp, row_off, n_seq, t_len, l, pool_w, pool_scale):
    base = row_off // t_len
    col0 = 2 * LRU_W // POOL_GW
    return pl.pallas_call(
        _pool_kernel,
        grid=(n_seq, len(POOL_WINDOWS)),
        in_specs=[pl.BlockSpec((t_len, POOL_GW), lambda s, g: (base + s, col0 + g)),
                  pl.BlockSpec((None, None, POOL_GW, POOL_GW), lambda s, g: (l, g, 0, 0)),
                  pl.BlockSpec((None, 1, POOL_GW), lambda s, g: (l, 0, g))],
        out_specs=pl.BlockSpec((t_len, POOL_GW), lambda s, g: (s, g)),
        out_shape=jax.ShapeDtypeStruct((n_seq * t_len, BRANCH_W), BF16),
        compiler_params=_params("arbitrary", "arbitrary"),
        name="pool_mixer",
    )(z_lp, pool_w, pool_scale.reshape(DEPTH, 1, BRANCH_W))


def _grid_position_code(n_tokens):
    rows = n_tokens // GRID_W
    rr, cc = jnp.meshgrid(jnp.arange(rows, dtype=F32), jnp.arange(GRID_W, dtype=F32), indexing="ij")
    quarter = D_MODEL // 4
    omega = POS_BASE ** (-jnp.arange(quarter, dtype=F32) / quarter)

    def enc(pos):
        ang = pos.reshape(-1)[:, None] * omega[None, :]
        return jnp.concatenate([jnp.sin(ang), jnp.cos(ang)], axis=-1)

    return jnp.concatenate([enc(rr), enc(cc)], axis=-1)


def kernel(x_prompt, x_sample, state_rwkv, state_lru, c, c_ctx, ada_w, ada_b, norms, w_in, fnet_w, rwkv_mu, rwkv_w0, rwkv_w2, rwkv_a0, rwkv_a2, rwkv_kk, rwkv_ka, rwkv_rk, rwkv_g2, rwkv_gn_w, rwkv_gn_b, rwkv_v0, rwkv_v1, rwkv_v2, lru_conv_w, lru_conv_b, lru_wa, lru_ba, lru_wx, lru_bx, lru_lam, pool_w, pool_scale, w_branch, w_out, router_w, router_b, moe_w_gate, moe_b_gate, moe_w_up, moe_b_up, moe_w_down, moe_b_down):
    x_lat = x_sample + _grid_position_code(DEC_SEQ).astype(x_sample.dtype)[None]
    x = jnp.concatenate([x_prompt.reshape(N_CTX, D_MODEL), x_lat.reshape(N_LAT, D_MODEL)], axis=0)

    cvec = jnp.zeros((8, D_MODEL), F32).at[0].set(c_ctx).at[1:1 + DEC_BATCH].set(c)
    mod = ada_modulation(cvec, ada_w, ada_b)[:, :N_GROUPS].reshape(DEPTH, N_GROUPS, 6, 1, D_MODEL)
    norms4 = norms.reshape(DEPTH, 4, 1, D_MODEL)

    groups = ((0, BATCH, SEQ), (N_CTX, DEC_BATCH, DEC_SEQ))
    s0_ctx = jnp.zeros((BATCH, 2, N_PAIR, LANES, LANES), F32)
    h0_ctx = jnp.zeros((BATCH, 2, LRU_W), F32)
    v_first = None
    S_ctx, h_ctx = [], []

    xn = norm_mod(x, norms4, mod, 0, 0, 1, 0)
    for l in range(DEPTH):
        z_f = matmul_cols(xn, w_in, (l,), 0, BRANCH_W, F32, "in_proj_fourier")
        z_r = matmul_cols(xn, w_in, (l,), BRANCH_W, RWKV_PAD, F32, "in_proj_rwkv")
        w_rest = w_in[l, :, REST_OFF:]
        z_lp = matmul_cols(xn, w_rest, (), 0, LP_W, F32, "in_proj_lru_pool")

        r, v, kk, g, bonus, lwf, kf, bf, lwb, kb, bb = rwkv_prep(
            z_r, v_first, l, N_CTX // MIX_TM, DEC_SEQ // MIX_TM, rwkv_mu, rwkv_w0, rwkv_w2, rwkv_a0, rwkv_a2,
            rwkv_kk, rwkv_ka, rwkv_rk, rwkv_g2, rwkv_v0, rwkv_v1, rwkv_v2)
        if l == 0:
            v_first = v
        scan_in = (r, v, kk, lwf, kf, bf, lwb, kb, bb)
        s0 = (s0_ctx, pair_states(state_rwkv[:, l]))
        h0 = (h0_ctx, state_lru[:, l])
        yf, yb, u_f, u_l, u_p = [], [], [], [], []
        for gi, (row0, n_seq, t_len) in enumerate(groups):
            yf_g, yb_g, s_g = wkv_scan(scan_in, s0[gi], row0, n_seq, t_len)
            yf.append(yf_g)
            yb.append(yb_g)
            u_f.append(fourier_mixer(z_f, row0, n_seq, t_len, l, fnet_w))
            u_l_g, h_g = rglru_mixer(z_lp, h0[gi], row0, n_seq, t_len, l, lru_conv_w, lru_conv_b, lru_wa,
                                     lru_ba, lru_wx, lru_bx, lru_lam)
            u_l.append(u_l_g)
            u_p.append(pool_mixer(z_lp, row0, n_seq, t_len, l, pool_w, pool_scale))
            if gi == 0:
                S_ctx.append(unpair_states(s_g))
                h_ctx.append(h_g)
        cat = lambda parts: jnp.concatenate(parts, axis=0)
        u_r = rwkv_post(cat(yf), cat(yb), bonus, g, rwkv_gn_w[l], rwkv_gn_b[l])
        u = jnp.concatenate([cat(u_f), u_r, cat(u_l), cat(u_p)], axis=1)
        gated = gated_branches(xn, u, w_rest, w_branch, l)
        mix = matmul_cols(gated, w_out, (l,), 0, D_MODEL, F32, "out_proj")
        x1, xn2 = resid_norm(x, mix, norms4, mod, l, 1, 2, l, 2, 4, 3)
        l2 = min(l + 1, DEPTH - 1)
        x, xn = moe_ffn(xn2, x1, norms4, mod, l, l2, router_w, router_b, moe_w_gate, moe_b_gate,
                        moe_w_up, moe_b_up, moe_w_down, moe_b_down)

    y_prompt = x[:N_CTX].reshape(BATCH, SEQ, D_MODEL)
    y_sample = x[N_CTX:].reshape(DEC_BATCH, DEC_SEQ, D_MODEL)
    return (y_prompt, y_sample, jnp.stack(S_ctx, axis=1), jnp.stack(h_ctx, axis=1))
```
